```python
import jax, jax.numpy as jnp
from jax import lax
import numpy as np

D_MODEL = 1024
BATCH = 8
SEQ = 4096
DEPTH = 2

CHUNK = 64
N_A_LAYERS = DEPTH // 2
N_B_LAYERS = DEPTH - N_A_LAYERS
N_DENSE = (DEPTH + 1) // 2
N_MOE = DEPTH // 2

MLSTM_HEADS = 8
MLSTM_QK_DIM = D_MODEL // 16
MLSTM_V_DIM = D_MODEL // 8
MLSTM_QK = MLSTM_HEADS * MLSTM_QK_DIM
MLSTM_V = MLSTM_HEADS * MLSTM_V_DIM
MLSTM_PROJ = 2 * MLSTM_QK + 2 * MLSTM_V + 2 * MLSTM_HEADS
GATE_SOFTCAP = 15.0

SB_HEADS = 16
SB_HEAD_DIM = D_MODEL // SB_HEADS
SB_WIDTH = SB_HEADS * SB_HEAD_DIM
Q_BLOCK = 128

FFN_DIM = 2816
N_EXPERTS = 8
TOP_K = 2
EXPERT_DIM = 3584
EPS = 1e-6

kernel_name = "yoco_mlstm_stickbreaking_moe_trunk"


def rmsnorm(x, g):
    xf = x.astype(jnp.float32)
    y = xf * lax.rsqrt(jnp.mean(xf * xf, axis=-1, keepdims=True) + EPS)
    return (y * g.astype(jnp.float32)).astype(x.dtype)


def head_rms(x, g):
    xf = x.astype(jnp.float32)
    return xf * lax.rsqrt(jnp.mean(xf * xf, axis=-1, keepdims=True) + EPS) * g.astype(jnp.float32)


def swiglu(x, w_gate, w_up, w_down):
    return (jax.nn.silu(x @ w_gate) * (x @ w_up)) @ w_down


def soft_cap(z):
    return GATE_SOFTCAP * jnp.tanh(z / GATE_SOFTCAP)


def mlstm_chunkwise(q, k, v, ig, fg):
    B, H, S, dk = q.shape
    dv = v.shape[-1]
    L = CHUNK
    NC = S // L
    q = q.reshape(B, H, NC, L, dk)
    k = k.reshape(B, H, NC, L, dk)
    v = v.reshape(B, H, NC, L, dv)
    ig = ig.reshape(B, H, NC, L)
    b = lax.cumsum(jax.nn.log_sigmoid(fg).reshape(B, H, NC, L), axis=3)
    b_last = b[..., -1]
    g = b_last[..., None] - b + ig
    g_max = jnp.max(g, axis=-1)
    w = jnp.exp(g - g_max[..., None])
    c_loc = jnp.einsum('bhclv,bhclk->bhcvk', v * w[..., None], k)
    n_loc = jnp.einsum('bhcl,bhclk->bhck', w, k)

    def step(carry, inp):
        c, n, m = carry
        cl, nl, bl, gm = inp
        m_new = jnp.maximum(bl + m, gm)
        a = jnp.exp(bl + m - m_new)
        e = jnp.exp(gm - m_new)
        c_new = a[..., None, None] * c + e[..., None, None] * cl
        n_new = a[..., None] * n + e[..., None] * nl
        return (c_new, n_new, m_new), (c, n, m)

    init = (jnp.zeros((B, H, dv, dk), jnp.float32),
            jnp.zeros((B, H, dk), jnp.float32),
            jnp.full((B, H), -jnp.inf, jnp.float32))
    xs = (jnp.moveaxis(c_loc, 2, 0), jnp.moveaxis(n_loc, 2, 0),
          jnp.moveaxis(b_last, 2, 0), jnp.moveaxis(g_max, 2, 0))
    _, (c_prev, n_prev, m_prev) = lax.scan(step, init, xs)
    c_prev = jnp.moveaxis(c_prev, 0, 2)
    n_prev = jnp.moveaxis(n_prev, 0, 2)
    m_prev = jnp.moveaxis(m_prev, 0, 2)

    causal = jnp.tril(jnp.ones((L, L), dtype=bool))
    log_d = jnp.where(causal, b[..., :, None] - b[..., None, :] + ig[..., None, :], -jnp.inf)
    a_inter = b + m_prev[..., None]
    m_t = jnp.maximum(a_inter, jnp.max(log_d, axis=-1))
    s = jnp.einsum('bhctk,bhcsk->bhcts', q, k) * jnp.exp(log_d - m_t[..., None])
    w_inter = jnp.exp(a_inter - m_t)
    num = (jnp.einsum('bhcts,bhcsv->bhctv', s, v)
           + w_inter[..., None] * jnp.einsum('bhctk,bhcvk->bhctv', q, c_prev))
    den = jnp.sum(s, axis=-1) + w_inter * jnp.einsum('bhctk,bhck->bhct', q, n_prev)
    h = num / jnp.maximum(jnp.abs(den), jnp.exp(-m_t))[..., None]
    return h.reshape(B, H, S, dv)


def mlstm_mixer(xn, w_in, b_igate, b_fgate, g_h, w_out):
    B, S, _ = xn.shape
    H = MLSTM_HEADS
    proj = xn @ w_in
    o0 = 0
    q = proj[..., o0:o0 + MLSTM_QK]; o0 += MLSTM_QK
    k = proj[..., o0:o0 + MLSTM_QK]; o0 += MLSTM_QK
    v = proj[..., o0:o0 + MLSTM_V]; o0 += MLSTM_V
    og = proj[..., o0:o0 + MLSTM_V]; o0 += MLSTM_V
    ig = proj[..., o0:o0 + H]; o0 += H
    fg = proj[..., o0:o0 + H]
    to_heads = lambda t, d: jnp.transpose(t.reshape(B, S, H, d), (0, 2, 1, 3)).astype(jnp.float32)
    q = to_heads(q, MLSTM_QK_DIM)
    k = to_heads(k, MLSTM_QK_DIM) * (MLSTM_QK_DIM ** -0.5)
    v = to_heads(v, MLSTM_V_DIM)
    ig = soft_cap(jnp.transpose(ig.astype(jnp.float32) + b_igate, (0, 2, 1)))
    fg = soft_cap(jnp.transpose(fg.astype(jnp.float32) + b_fgate, (0, 2, 1)))
    h = mlstm_chunkwise(q, k, v, ig, fg)
    h = head_rms(h, jnp.ones((), jnp.float32))
    h = jnp.transpose(h, (0, 2, 1, 3)).reshape(B, S, MLSTM_V) * g_h.astype(jnp.float32)
    h = h * jax.nn.sigmoid(og.astype(jnp.float32))
    return h.astype(xn.dtype) @ w_out


def stick_breaking(q, k, v):
    B, H, S, d = q.shape
    NB = S // Q_BLOCK
    qb = jnp.moveaxis(q.reshape(B, H, NB, Q_BLOCK, d), 2, 0)
    key_pos = jnp.arange(S)

    def block(args):
        qi, t0 = args
        z = jnp.einsum('bhtd,bhsd->bhts', qi, k) * (d ** -0.5)
        t_pos = t0 + jnp.arange(Q_BLOCK)
        mask = key_pos[None, :] < t_pos[:, None]
        log_beta = jax.nn.log_sigmoid(z)
        log_1m = jnp.where(mask, log_beta - z, 0.0)
        between = lax.cumsum(log_1m, axis=3, reverse=True) - log_1m
        a = jnp.where(mask, jnp.exp(log_beta + between), 0.0)
        return jnp.einsum('bhts,bhsd->bhtd', a, v)

    out = lax.map(block, (qb, jnp.arange(NB, dtype=jnp.int32) * Q_BLOCK))
    return jnp.moveaxis(out, 0, 2).reshape(B, H, S, d)


def shared_kv(x, kv_norm, w_kv, g_k):
    B, S, _ = x.shape
    kv = rmsnorm(x, kv_norm) @ w_kv
    k = kv[..., :SB_WIDTH].reshape(B, S, SB_HEADS, SB_HEAD_DIM)
    v = kv[..., SB_WIDTH:].reshape(B, S, SB_HEADS, SB_HEAD_DIM)
    k = jnp.transpose(head_rms(k, g_k), (0, 2, 1, 3))
    v = jnp.transpose(v.astype(jnp.float32), (0, 2, 1, 3))
    return k, v


def sb_mixer(xn, k, v, w_q, g_q, w_o):
    B, S, _ = xn.shape
    q = (xn @ w_q).reshape(B, S, SB_HEADS, SB_HEAD_DIM)
    q = jnp.transpose(head_rms(q, g_q), (0, 2, 1, 3))
    o = stick_breaking(q, k, v)
    o = jnp.transpose(o, (0, 2, 1, 3)).reshape(B, S, SB_WIDTH).astype(xn.dtype)
    return o @ w_o


def moe_swiglu(xn, w_router, w_gate, w_up, w_down):
    B, S, D = xn.shape
    t = xn.reshape(B * S, D)
    logits = (t @ w_router).astype(jnp.float32)
    top_val, top_idx = lax.top_k(logits, TOP_K)
    top_w = jax.nn.softmax(top_val, axis=-1)
    combine = jnp.sum(jax.nn.one_hot(top_idx, N_EXPERTS, dtype=jnp.float32) * top_w[..., None], axis=1)
    combine = combine.astype(t.dtype)
    y = jnp.zeros_like(t)
    for e in range(N_EXPERTS):
        y = y + combine[:, e:e + 1] * swiglu(t, w_gate[e], w_up[e], w_down[e])
    return y.reshape(B, S, D)


def _normal(k, shape, scale):
    return jax.random.normal(k, shape, jnp.float32) * scale


def setup_inputs(seed: int = 0) -> dict:
    key = jax.random.key(seed)
    ks = jax.random.split(key, 24)
    D = D_MODEL
    return {
        "x": _normal(ks[0], (BATCH, SEQ, D), 1.0),
        "mix_norm": 1.0 + _normal(ks[1], (DEPTH, D), 0.05),
        "ffn_norm": 1.0 + _normal(ks[2], (DEPTH, D), 0.05),
        "mlstm_w_in": _normal(ks[3], (N_A_LAYERS, D, MLSTM_PROJ), D ** -0.5),
        "mlstm_b_igate": _normal(ks[4], (N_A_LAYERS, MLSTM_HEADS), 0.1),
        "mlstm_b_fgate": 3.0 + _normal(ks[5], (N_A_LAYERS, MLSTM_HEADS), 0.5),
        "mlstm_g_h": 1.0 + _normal(ks[6], (N_A_LAYERS, MLSTM_V), 0.05),
        "mlstm_w_out": _normal(ks[7], (N_A_LAYERS, MLSTM_V, D), MLSTM_V ** -0.5),
        "kv_norm": 1.0 + _normal(ks[8], (D,), 0.05),
        "w_kv": _normal(ks[9], (D, 2 * SB_WIDTH), D ** -0.5),
        "g_k": 1.0 + _normal(ks[10], (SB_HEAD_DIM,), 0.05),
        "sb_w_q": _normal(ks[11], (N_B_LAYERS, D, SB_WIDTH), D ** -0.5),
        "sb_g_q": 1.0 + _normal(ks[12], (N_B_LAYERS, SB_HEAD_DIM), 0.05),
        "sb_w_o": _normal(ks[13], (N_B_LAYERS, SB_WIDTH, D), SB_WIDTH ** -0.5),
        "ffn_w_gate": _normal(ks[14], (N_DENSE, D, FFN_DIM), D ** -0.5),
        "ffn_w_up": _normal(ks[15], (N_DENSE, D, FFN_DIM), D ** -0.5),
        "ffn_w_down": _normal(ks[16], (N_DENSE, FFN_DIM, D), FFN_DIM ** -0.5),
        "moe_w_router": _normal(ks[17], (N_MOE, D, N_EXPERTS), D ** -0.5),
        "moe_w_gate": _normal(ks[18], (N_MOE, N_EXPERTS, D, EXPERT_DIM), D ** -0.5),
        "moe_w_up": _normal(ks[19], (N_MOE, N_EXPERTS, D, EXPERT_DIM), D ** -0.5),
        "moe_w_down": _normal(ks[20], (N_MOE, N_EXPERTS, EXPERT_DIM, D), EXPERT_DIM ** -0.5),
    }


def reference(x, mix_norm, ffn_norm, mlstm_w_in, mlstm_b_igate, mlstm_b_fgate, mlstm_g_h,
              mlstm_w_out, kv_norm, w_kv, g_k, sb_w_q, sb_g_q, sb_w_o, ffn_w_gate, ffn_w_up,
              ffn_w_down, moe_w_router, moe_w_gate, moe_w_up, moe_w_down):
    k_sh = None
    v_sh = None
    for layer in range(DEPTH):
        xn = rmsnorm(x, mix_norm[layer])
        if layer < N_A_LAYERS:
            x = x + mlstm_mixer(xn, mlstm_w_in[layer], mlstm_b_igate[layer], mlstm_b_fgate[layer],
                                mlstm_g_h[layer], mlstm_w_out[layer])
        else:
            if layer == N_A_LAYERS:
                k_sh, v_sh = shared_kv(x, kv_norm, w_kv, g_k)
                xn = rmsnorm(x, mix_norm[layer])
            j = layer - N_A_LAYERS
            x = x + sb_mixer(xn, k_sh, v_sh, sb_w_q[j], sb_g_q[j], sb_w_o[j])
        xn = rmsnorm(x, ffn_norm[layer])
        if layer % 2 == 0:
            i = layer // 2
            x = x + swiglu(xn, ffn_w_gate[i], ffn_w_up[i], ffn_w_down[i])
        else:
            i = layer // 2
            x = x + moe_swiglu(xn, moe_w_router[i], moe_w_gate[i], moe_w_up[i], moe_w_down[i])
    return x
```

```python
import functools

import jax
import jax.numpy as jnp
from jax import lax
from jax.experimental import pallas as pl
from jax.experimental.pallas import tpu as pltpu

F32 = jnp.float32
BF16 = jnp.bfloat16

EPS = 1e-6
GATE_SOFTCAP = 15.0
MLSTM_HEADS = 8
MLSTM_QK_DIM = 64
MLSTM_V_DIM = 128
SB_HEADS = 16
SB_HEAD_DIM = 64
N_EXPERTS = 8

V7X_LANES = 128
V7X_VMEM_BYTES = 64 * 1024 * 1024

ROW_TILE = 512
FFN_ROW_TILE = 1024
FFN_COL_TILE = 256
MLSTM_TILE = 256
SB_TILE = 256


def _vmem_limit(nbytes):
    return int(min(max(2 * nbytes, 32 * 1024 * 1024), V7X_VMEM_BYTES - 8 * 1024 * 1024))


def _params(semantics, nbytes):
    return pltpu.CompilerParams(dimension_semantics=semantics, vmem_limit_bytes=_vmem_limit(nbytes))


def _rms_scale(x, g):
    ms = jnp.mean(x * x, axis=-1, keepdims=True)
    return x * lax.rsqrt(ms + EPS) * g


def _norm_matmul_kernel(x_ref, g_ref, w_ref, o_ref, xn_ref):
    @pl.when(pl.program_id(1) == 0)
    def _():
        xn_ref[...] = _rms_scale(x_ref[...], g_ref[...]).astype(BF16)

    o_ref[...] = jnp.dot(xn_ref[...], w_ref[...], preferred_element_type=F32).astype(o_ref.dtype)


def norm_matmul(x, g, w, out_dtype, tn):
    t, d = x.shape
    n = w.shape[1]
    tm = ROW_TILE
    nbytes = 2 * (tm * d * 4 + d * tn * 2 + tm * tn * 4) + tm * d * 2
    return pl.pallas_call(
        _norm_matmul_kernel,
        grid=(t // tm, n // tn),
        in_specs=[
            pl.BlockSpec((tm, d), lambda i, j: (i, 0)),
            pl.BlockSpec((1, d), lambda i, j: (0, 0)),
            pl.BlockSpec((d, tn), lambda i, j: (0, j)),
        ],
        out_specs=pl.BlockSpec((tm, tn), lambda i, j: (i, j)),
        out_shape=jax.ShapeDtypeStruct((t, n), out_dtype),
        scratch_shapes=[pltpu.VMEM((tm, d), BF16)],
        compiler_params=_params(("parallel", "arbitrary"), nbytes),
        name="norm_matmul",
    )(x, g.reshape(1, d), w)


def _matmul_residual_kernel(a_ref, w_ref, r_ref, o_ref):
    o_ref[...] = r_ref[...] + jnp.dot(a_ref[...], w_ref[...], preferred_element_type=F32)


def matmul_residual(a, w, res):
    t, k = a.shape
    n = w.shape[1]
    tm = ROW_TILE
    nbytes = 2 * (tm * k * 2 + k * n * 2 + 2 * tm * n * 4)
    return pl.pallas_call(
        _matmul_residual_kernel,
        grid=(t // tm,),
        in_specs=[
            pl.BlockSpec((tm, k), lambda i: (i, 0)),
            pl.BlockSpec((k, n), lambda i: (0, 0)),
            pl.BlockSpec((tm, n), lambda i: (i, 0)),
        ],
        out_specs=pl.BlockSpec((tm, n), lambda i: (i, 0)),
        out_shape=jax.ShapeDtypeStruct((t, n), F32),
        compiler_params=_params(("parallel",), nbytes),
        name="matmul_residual",
    )(a, w, res)


def _split3(x):
    hi = x.astype(BF16)
    r1 = x - hi.astype(F32)
    mid = r1.astype(BF16)
    lo = (r1 - mid.astype(F32)).astype(BF16)
    return hi, mid, lo


def _gate_prep_kernel(g_ref, b_ref, tri_ref, ig_ref, bc_ref):
    z = g_ref[...] + b_ref[...]
    z = GATE_SOFTCAP * jnp.tanh(z / GATE_SOFTCAP)
    ig_ref[...] = z
    lf = jax.nn.log_sigmoid(z)
    hi, mid, lo = _split3(lf)
    parts = jnp.dot(tri_ref[...], jnp.concatenate([hi, mid, lo], axis=1), preferred_element_type=F32)
    w = V7X_LANES
    bc_ref[...] = (parts[:, :w] + parts[:, w:2 * w]) + parts[:, 2 * w:]


def gate_prep(graw, bias):
    t, w = graw.shape
    tl = MLSTM_TILE
    r = lax.broadcasted_iota(jnp.int32, (tl, tl), 0)
    c = lax.broadcasted_iota(jnp.int32, (tl, tl), 1)
    tri = (c <= r).astype(BF16)
    spec = pl.BlockSpec((tl, w), lambda i: (i, 0))
    return pl.pallas_call(
        _gate_prep_kernel,
        grid=(t // tl,),
        in_specs=[spec, pl.BlockSpec((1, w), lambda i: (0, 0)), pl.BlockSpec((tl, tl), lambda i: (0, 0))],
        out_specs=[spec, spec],
        out_shape=[jax.ShapeDtypeStruct((t, w), F32)] * 2,
        compiler_params=_params(("parallel",), 8 * tl * w * 4),
        name="gate_prep",
    )(graw, bias, tri)


def _mlstm_kernel(q_ref, k_ref, v_ref, og_ref, ig_ref, bc_ref, gh_ref, o_ref, state_ref):
    head = pl.program_id(1)
    tl = q_ref.shape[0]
    dv = v_ref.shape[1]

    @pl.when(pl.program_id(2) == 0)
    def _():
        state_ref[...] = jnp.zeros_like(state_ref)

    lane = lax.broadcasted_iota(jnp.int32, (tl, V7X_LANES), 1)
    mine = (lane // MLSTM_QK_DIM) == (head % 2)
    zero = jnp.zeros((), BF16)
    q = jnp.where(mine, q_ref[...], zero) * jnp.asarray(MLSTM_QK_DIM ** -0.5, BF16)
    k = jnp.where(mine, k_ref[...], zero)
    v = v_ref[...]

    ig = jnp.sum(jnp.where(lane == head, ig_ref[...], 0.0), axis=-1, keepdims=True)
    bc = jnp.sum(jnp.where(lane == head + MLSTM_HEADS, bc_ref[...], 0.0), axis=-1, keepdims=True)
    b_last = bc[tl - 1:tl, :]

    src = jnp.broadcast_to(bc - ig, (tl, V7X_LANES)).T[0:1, :]
    row = lax.broadcasted_iota(jnp.int32, (tl, tl), 0)
    col = lax.broadcasted_iota(jnp.int32, (tl, tl), 1)
    decay = jnp.exp(jnp.where(col <= row, bc - src, -jnp.inf))

    s = lax.dot_general(q, k, (((1,), (1,)), ((), ())), preferred_element_type=F32)
    p = (s * decay).astype(BF16)
    v_aug = jnp.concatenate([v, jnp.ones((tl, dv), BF16)], axis=1)
    st = state_ref[...]
    r = jnp.dot(p, v_aug, preferred_element_type=F32)
    r = r + jnp.exp(bc) * jnp.dot(q, st.astype(BF16), preferred_element_type=F32)
    num = r[:, :dv]
    den = r[:, dv:]
    h = num / jnp.maximum(jnp.abs(den), 1.0)
    h = h * lax.rsqrt(jnp.mean(h * h, axis=-1, keepdims=True) + EPS)
    h = h * gh_ref[...] * jax.nn.sigmoid(og_ref[...].astype(F32))
    o_ref[...] = h.astype(o_ref.dtype)

    w = jnp.exp(b_last - bc + ig)
    vw = jnp.concatenate([v.astype(F32) * w, jnp.broadcast_to(w, (tl, dv))], axis=1).astype(BF16)
    upd = lax.dot_general(k, vw, (((0,), (0,)), ((), ())), preferred_element_type=F32)
    state_ref[...] = jnp.exp(b_last) * st + upd


def mlstm_core(proj, igs, bcum, g_h, batch, seq):
    t = proj.shape[0]
    tl = MLSTM_TILE
    nt = seq // tl
    heads = MLSTM_HEADS
    dv = MLSTM_V_DIM
    w = V7X_LANES
    q_blocks = heads * MLSTM_QK_DIM // w
    v_block0 = 2 * q_blocks
    o_block0 = v_block0 + heads

    def rows(b, h, i):
        return b * nt + i

    nbytes = 2 * (4 * tl * w * 2 + 2 * tl * w * 4 + tl * dv * 2) + w * 2 * dv * 4 + 8 * tl * tl * 4
    return pl.pallas_call(
        _mlstm_kernel,
        grid=(batch, heads, nt),
        in_specs=[
            pl.BlockSpec((tl, w), lambda b, h, i: (rows(b, h, i), h // 2)),
            pl.BlockSpec((tl, w), lambda b, h, i: (rows(b, h, i), q_blocks + h // 2)),
            pl.BlockSpec((tl, dv), lambda b, h, i: (rows(b, h, i), v_block0 + h)),
            pl.BlockSpec((tl, dv), lambda b, h, i: (rows(b, h, i), o_block0 + h)),
            pl.BlockSpec((tl, w), lambda b, h, i: (rows(b, h, i), 0)),
            pl.BlockSpec((tl, w), lambda b, h, i: (rows(b, h, i), 0)),
            pl.BlockSpec((1, dv), lambda b, h, i: (0, h)),
        ],
        out_specs=pl.BlockSpec((tl, dv), lambda b, h, i: (rows(b, h, i), h)),
        out_shape=jax.ShapeDtypeStruct((t, heads * dv), BF16),
        scratch_shapes=[pltpu.VMEM((w, 2 * dv), F32)],
        compiler_params=_params(("parallel", "parallel", "arbitrary"), nbytes),
        name="mlstm_core",
    )(proj, proj, proj, proj, igs, bcum, g_h.reshape(1, heads * dv))


def _key_proj_kernel(x_ref, g_ref, w_ref, gk_ref, o_ref, xn_ref):
    @pl.when(pl.program_id(1) == 0)
    def _():
        xn_ref[...] = _rms_scale(x_ref[...], g_ref[...]).astype(BF16)

    y = jnp.dot(xn_ref[...], w_ref[...], preferred_element_type=F32)
    d = SB_HEAD_DIM
    for c in range(y.shape[1] // V7X_LANES):
        yt = y[:, c * V7X_LANES:(c + 1) * V7X_LANES].T
        for half in range(V7X_LANES // d):
            blk = yt[half * d:(half + 1) * d, :]
            ms = jnp.mean(blk * blk, axis=0, keepdims=True)
            kn = blk * lax.rsqrt(ms + EPS) * gk_ref[...]
            o_ref[0, c, half * d:(half + 1) * d, :] = kn.astype(o_ref.dtype)


def key_proj(x, g, w, g_k, batch, seq):
    t, d = x.shape
    n = w.shape[1]
    tm = ROW_TILE
    tn = 512
    ns = seq // tm
    pairs = n // V7X_LANES
    nbytes = 2 * (tm * d * 4 + d * tn * 2 + tm * tn * 2) + tm * d * 2 + 2 * tm * tn * 4
    return pl.pallas_call(
        _key_proj_kernel,
        grid=(t // tm, n // tn),
        in_specs=[
            pl.BlockSpec((tm, d), lambda i, j: (i, 0)),
            pl.BlockSpec((1, d), lambda i, j: (0, 0)),
            pl.BlockSpec((d, tn), lambda i, j: (0, j)),
            pl.BlockSpec((SB_HEAD_DIM, 1), lambda i, j: (0, 0)),
        ],
        out_specs=pl.BlockSpec((1, tn // V7X_LANES, V7X_LANES, tm), lambda i, j: (i // ns, j, 0, i % ns)),
        out_shape=jax.ShapeDtypeStruct((batch, pairs, V7X_LANES, seq), BF16),
        scratch_shapes=[pltpu.VMEM((tm, d), BF16)],
        compiler_params=_params(("parallel", "arbitrary"), nbytes),
        name="key_proj",
    )(x, g.reshape(1, d), w, g_k.reshape(SB_HEAD_DIM, 1))


def _softplus(z):
    return jnp.maximum(z, 0.0) + jnp.log1p(jnp.exp(-jnp.abs(z)))


def _sb_block(qh, kt, vb, upper, run, masked):
    tq, tk = qh.shape[0], kt.shape[1]
    z = jnp.dot(qh, kt, preferred_element_type=F32)
    log_1m = -_softplus(z)
    if masked:
        row = lax.broadcasted_iota(jnp.int32, (tq, tk), 0)
        col = lax.broadcasted_iota(jnp.int32, (tq, tk), 1)
        valid = col < row
        log_1m = jnp.where(valid, log_1m, 0.0)
    hi = log_1m.astype(BF16)
    lo = (log_1m - hi.astype(F32)).astype(BF16)
    both = jnp.dot(jnp.concatenate([hi, lo], axis=0), upper, preferred_element_type=F32)
    between = both[:tq] + both[tq:]
    a = jnp.exp((log_1m + z) + between + run)
    if masked:
        a = jnp.where(valid, a, 0.0)
    out = jnp.dot(a.astype(BF16), vb, preferred_element_type=F32)
    return out, run + jnp.sum(log_1m, axis=-1, keepdims=True)


def _sb_kernel(q_ref, gq_ref, kt_ref, v_ref, up_ref, o_ref):
    i = pl.program_id(2)
    tq = q_ref.shape[0]
    tk = up_ref.shape[0]
    d = SB_HEAD_DIM
    x = q_ref[...]
    lane = lax.broadcasted_iota(jnp.int32, x.shape, 1)
    first = lane < d
    x2 = x * x
    ss_a = jnp.sum(jnp.where(first, x2, 0.0), axis=-1, keepdims=True)
    ss_b = jnp.sum(jnp.where(first, 0.0, x2), axis=-1, keepdims=True)
    inv = jnp.where(first, lax.rsqrt(ss_a / d + EPS), lax.rsqrt(ss_b / d + EPS))
    qn = x * inv * gq_ref[...] * (d ** -0.5)
    upper = up_ref[...]

    outs = []
    for sel in (first, jnp.logical_not(first)):
        qh = jnp.where(sel, qn, 0.0).astype(BF16)

        def block(kb, qh=qh):
            start = pl.multiple_of(kb * tk, tk)
            return kt_ref[0, 0, :, pl.ds(start, tk)], v_ref[pl.ds(start, tk), :]

        kt, vb = block(i)
        acc, run = _sb_block(qh, kt, vb, upper, jnp.zeros((tq, 1), F32), True)

        def body(j, carry, qh=qh, block=block):
            acc, run = carry
            kt, vb = block(i - j)
            out, run = _sb_block(qh, kt, vb, upper, run, False)
            return acc + out, run

        acc, _ = lax.fori_loop(1, i + 1, body, (acc, run))
        outs.append(acc)
    o_ref[...] = jnp.where(first, outs[0], outs[1]).astype(o_ref.dtype)


def stick_breaking(qraw, g_q, kt, v, batch, seq):
    t, width = qraw.shape
    tq = SB_TILE
    nq = seq // tq
    w = V7X_LANES
    pairs = width // w
    r = lax.broadcasted_iota(jnp.int32, (tq, tq), 0)
    c = lax.broadcasted_iota(jnp.int32, (tq, tq), 1)
    upper = (r > c).astype(BF16)
    gq2 = jnp.tile(g_q.reshape(1, SB_HEAD_DIM), (1, w // SB_HEAD_DIM))
    nbytes = 2 * (tq * w * 4 + 2 * seq * w * 2 + tq * tq * 2 + tq * w * 2) + 16 * tq * tq * 4
    return pl.pallas_call(
        _sb_kernel,
        grid=(batch, pairs, nq),
        in_specs=[
            pl.BlockSpec((tq, w), lambda b, p, i: (b * nq + i, p)),
            pl.BlockSpec((1, w), lambda b, p, i: (0, 0)),
            pl.BlockSpec((1, 1, w, seq), lambda b, p, i: (b, p, 0, 0)),
            pl.BlockSpec((seq, w), lambda b, p, i: (b, p)),
            pl.BlockSpec((tq, tq), lambda b, p, i: (0, 0)),
        ],
        out_specs=pl.BlockSpec((tq, w), lambda b, p, i: (b * nq + i, p)),
        out_shape=jax.ShapeDtypeStruct((t, width), BF16),
        compiler_params=_params(("parallel", "parallel", "arbitrary"), nbytes),
        name="stick_breaking",
    )(qraw, gq2, kt, v, upper)


def _ffn_kernel(x_ref, g_ref, wg_ref, wu_ref, wd_ref, o_ref, xn_ref, acc_ref):
    f = pl.program_id(1)

    @pl.when(f == 0)
    def _():
        xn_ref[...] = _rms_scale(x_ref[...], g_ref[...]).astype(BF16)
        acc_ref[...] = jnp.zeros_like(acc_ref)

    xn = xn_ref[...]
    hg = jnp.dot(xn, wg_ref[...], preferred_element_type=F32)
    hu = jnp.dot(xn, wu_ref[...], preferred_element_type=F32)
    act = (jax.nn.silu(hg) * hu).astype(BF16)
    acc_ref[...] += jnp.dot(act, wd_ref[...], preferred_element_type=F32)

    @pl.when(f == pl.num_programs(1) - 1)
    def _():
        o_ref[...] = x_ref[...] + acc_ref[...]


def ffn_dense(x, g, wg, wu, wd):
    t, d = x.shape
    f = wg.shape[1]
    tm = FFN_ROW_TILE
    tf = FFN_COL_TILE
    nbytes = 2 * (2 * tm * d * 4 + 3 * d * tf * 2) + tm * d * 6 + 4 * tm * tf * 4
    return pl.pallas_call(
        _ffn_kernel,
        grid=(t // tm, f // tf),
        in_specs=[
            pl.BlockSpec((tm, d), lambda i, j: (i, 0)),
            pl.BlockSpec((1, d), lambda i, j: (0, 0)),
            pl.BlockSpec((d, tf), lambda i, j: (0, j)),
            pl.BlockSpec((d, tf), lambda i, j: (0, j)),
            pl.BlockSpec((tf, d), lambda i, j: (j, 0)),
        ],
        out_specs=pl.BlockSpec((tm, d), lambda i, j: (i, 0)),
        out_shape=jax.ShapeDtypeStruct((t, d), F32),
        scratch_shapes=[pltpu.VMEM((tm, d), BF16), pltpu.VMEM((tm, d), F32)],
        compiler_params=_params(("parallel", "arbitrary"), nbytes),
        name="ffn_dense",
    )(x, g.reshape(1, d), wg, wu, wd)


def _router_kernel(x_ref, g_ref, wr_ref, xn_ref, comb_ref):
    xn = _rms_scale(x_ref[...], g_ref[...]).astype(BF16)
    xn_ref[...] = xn
    logits = jnp.dot(xn, wr_ref[...], preferred_element_type=F32)
    lane = lax.broadcasted_iota(jnp.int32, logits.shape, 1)
    neg = jnp.asarray(-jnp.inf, F32)
    logits = jnp.where(lane < N_EXPERTS, logits, neg)
    m1 = jnp.max(logits, axis=-1, keepdims=True)
    i1 = jnp.min(jnp.where(logits == m1, lane, V7X_LANES), axis=-1, keepdims=True)
    rest = jnp.where(lane == i1, neg, logits)
    m2 = jnp.max(rest, axis=-1, keepdims=True)
    i2 = jnp.min(jnp.where(rest == m2, lane, V7X_LANES), axis=-1, keepdims=True)
    e2 = jnp.exp(m2 - m1)
    denom = 1.0 + e2
    comb_ref[...] = jnp.where(lane == i1, 1.0 / denom, 0.0) + jnp.where(lane == i2, e2 / denom, 0.0)


def router(x, g, w_router_padded):
    t, d = x.shape
    tm = ROW_TILE
    w = V7X_LANES
    nbytes = 2 * (tm * d * 4 + d * w * 2 + tm * d * 2 + tm * w * 4) + 2 * tm * d * 4
    return pl.pallas_call(
        _router_kernel,
        grid=(t // tm,),
        in_specs=[
            pl.BlockSpec((tm, d), lambda i: (i, 0)),
            pl.BlockSpec((1, d), lambda i: (0, 0)),
            pl.BlockSpec((d, w), lambda i: (0, 0)),
        ],
        out_specs=[pl.BlockSpec((tm, d), lambda i: (i, 0)), pl.BlockSpec((tm, w), lambda i: (i, 0))],
        out_shape=[jax.ShapeDtypeStruct((t, d), BF16), jax.ShapeDtypeStruct((t, w), F32)],
        compiler_params=_params(("parallel",), nbytes),
        name="router",
    )(x, g.reshape(1, d), w_router_padded)


def _moe_kernel(x_ref, xn_ref, comb_ref, wg_ref, wu_ref, wd_ref, o_ref, y_ref, acc_ref):
    e = pl.program_id(1)
    f = pl.program_id(2)
    last_f = pl.num_programs(2) - 1

    @pl.when(jnp.logical_and(e == 0, f == 0))
    def _():
        y_ref[...] = x_ref[...]

    @pl.when(f == 0)
    def _():
        acc_ref[...] = jnp.zeros_like(acc_ref)

    xn = xn_ref[...]
    hg = jnp.dot(xn, wg_ref[0], preferred_element_type=F32)
    hu = jnp.dot(xn, wu_ref[0], preferred_element_type=F32)
    act = (jax.nn.silu(hg) * hu).astype(BF16)
    acc_ref[...] += jnp.dot(act, wd_ref[0], preferred_element_type=F32)

    @pl.when(f == last_f)
    def _():
        lane = lax.broadcasted_iota(jnp.int32, comb_ref.shape, 1)
        c = jnp.sum(jnp.where(lane == e, comb_ref[...], 0.0), axis=-1, keepdims=True)
        y_ref[...] += c * acc_ref[...]

    @pl.when(jnp.logical_and(e == pl.num_programs(1) - 1, f == last_f))
    def _():
        o_ref[...] = y_ref[...]


def moe_dense(x, xn, comb, wg, wu, wd):
    t, d = x.shape
    ne, _, fe = wg.shape
    tm = FFN_ROW_TILE
    tf = FFN_COL_TILE
    w = comb.shape[1]
    nbytes = 2 * (2 * tm * d * 4 + tm * d * 2 + tm * w * 4 + 3 * d * tf * 2) + 2 * tm * d * 4 + 4 * tm * tf * 4
    return pl.pallas_call(
        _moe_kernel,
        grid=(t // tm, ne, fe // tf),
        in_specs=[
            pl.BlockSpec((tm, d), lambda i, e, j: (i, 0)),
            pl.BlockSpec((tm, d), lambda i, e, j: (i, 0)),
            pl.BlockSpec((tm, w), lambda i, e, j: (i, 0)),
            pl.BlockSpec((1, d, tf), lambda i, e, j: (e, 0, j)),
            pl.BlockSpec((1, d, tf), lambda i, e, j: (e, 0, j)),
            pl.BlockSpec((1, tf, d), lambda i, e, j: (e, j, 0)),
        ],
        out_specs=pl.BlockSpec((tm, d), lambda i, e, j: (i, 0)),
        out_shape=jax.ShapeDtypeStruct((t, d), F32),
        scratch_shapes=[pltpu.VMEM((tm, d), F32), pltpu.VMEM((tm, d), F32)],
        compiler_params=_params(("parallel", "arbitrary", "arbitrary"), nbytes),
        name="moe_dense",
    )(x, xn, comb, wg, wu, wd)


def kernel(x, mix_norm, ffn_norm, mlstm_w_in, mlstm_b_igate, mlstm_b_fgate, mlstm_g_h, mlstm_w_out, kv_norm, w_kv, g_k, sb_w_q, sb_g_q, sb_w_o, ffn_w_gate, ffn_w_up, ffn_w_down, moe_w_router, moe_w_gate, moe_w_up, moe_w_down):
    batch, seq, d = x.shape
    xt = x.reshape(batch * seq, d)
    w = V7X_LANES
    n_main = 2 * MLSTM_HEADS * MLSTM_QK_DIM + 2 * MLSTM_HEADS * MLSTM_V_DIM

    w_in = mlstm_w_in[0]
    w_main = w_in[:, :n_main].astype(BF16)
    w_gate = jnp.pad(w_in[:, n_main:], ((0, 0), (0, w - 2 * MLSTM_HEADS))).astype(BF16)
    bias = jnp.pad(jnp.concatenate([mlstm_b_igate[0], mlstm_b_fgate[0]]), (0, w - 2 * MLSTM_HEADS)).reshape(1, w)
    proj = norm_matmul(xt, mix_norm[0], w_main, BF16, tn=512)
    graw = norm_matmul(xt, mix_norm[0], w_gate, F32, tn=w)
    igs, bcum = gate_prep(graw, bias)
    hmix = mlstm_core(proj, igs, bcum, mlstm_g_h[0], batch, seq)
    xt = matmul_residual(hmix, mlstm_w_out[0].astype(BF16), xt)

    xt = ffn_dense(xt, ffn_norm[0], ffn_w_gate[0].astype(BF16), ffn_w_up[0].astype(BF16), ffn_w_down[0].astype(BF16))

    sb_width = SB_HEADS * SB_HEAD_DIM
    kt = key_proj(xt, kv_norm, w_kv[:, :sb_width].astype(BF16), g_k, batch, seq)
    v = norm_matmul(xt, kv_norm, w_kv[:, sb_width:].astype(BF16), BF16, tn=512)
    qraw = norm_matmul(xt, mix_norm[1], sb_w_q[0].astype(BF16), F32, tn=512)
    o = stick_breaking(qraw, sb_g_q[0], kt, v, batch, seq)
    xt = matmul_residual(o, sb_w_o[0].astype(BF16), xt)

    w_r = jnp.pad(moe_w_router[0], ((0, 0), (0, w - N_EXPERTS))).astype(BF16)
    xn, comb = router(xt, ffn_norm[1], w_r)
    xt = moe_dense(xt, xn, comb, moe_w_gate[0].astype(BF16), moe_w_up[0].astype(BF16), moe_w_down[0].astype(BF16))
    return xt.reshape(batch, seq, d)
```

```python
import functools

import jax
import jax.numpy as jnp
from jax import lax
from jax.experimental import pallas as pl
from jax.experimental.pallas import tpu as pltpu

F32 = jnp.float32
BF16 = jnp.bfloat16

EPS = 1e-6
GATE_SOFTCAP = 15.0
MLSTM_HEADS = 8
MLSTM_QK_DIM = 64
MLSTM_V_DIM = 128
SB_HEADS = 16
SB_HEAD_DIM = 64
N_EXPERTS = 8

V7X_LANES = 128
V7X_VMEM_BYTES = 64 * 1024 * 1024

ROW_TILE = 512
FFN_ROW_TILE = 1024
FFN_COL_TILE = 256
MLSTM_TILE = 256
MOE_GROUP_TILE = 1024
MOE_ROUTE_TILE = 256
SB_Q_TILE = 512
SB_K_TILE = 256


def _vmem_limit(nbytes):
    return int(min(max(2 * nbytes, 32 * 1024 * 1024), V7X_VMEM_BYTES - 8 * 1024 * 1024))


def _params(semantics, nbytes):
    return pltpu.CompilerParams(dimension_semantics=semantics, vmem_limit_bytes=_vmem_limit(nbytes))


def _rms_scale(x, g):
    ms = jnp.mean(x * x, axis=-1, keepdims=True)
    return x * lax.rsqrt(ms + EPS) * g


def _norm_matmul_kernel(x_ref, g_ref, w_ref, o_ref, xn_ref):
    @pl.when(pl.program_id(1) == 0)
    def _():
        xn_ref[...] = _rms_scale(x_ref[...], g_ref[...]).astype(BF16)

    o_ref[...] = jnp.dot(xn_ref[...], w_ref[...], preferred_element_type=F32).astype(o_ref.dtype)


def norm_matmul(x, g, w, out_dtype, tn):
    t, d = x.shape
    n = w.shape[1]
    tm = ROW_TILE
    nbytes = 2 * (tm * d * 4 + d * tn * 2 + tm * tn * 4) + tm * d * 2
    return pl.pallas_call(
        _norm_matmul_kernel,
        grid=(t // tm, n // tn),
        in_specs=[
            pl.BlockSpec((tm, d), lambda i, j: (i, 0)),
            pl.BlockSpec((1, d), lambda i, j: (0, 0)),
            pl.BlockSpec((d, tn), lambda i, j: (0, j)),
        ],
        out_specs=pl.BlockSpec((tm, tn), lambda i, j: (i, j)),
        out_shape=jax.ShapeDtypeStruct((t, n), out_dtype),
        scratch_shapes=[pltpu.VMEM((tm, d), BF16)],
        compiler_params=_params(("parallel", "arbitrary"), nbytes),
        name="norm_matmul",
    )(x, g.reshape(1, d), w)


def _matmul_residual_kernel(a_ref, w_ref, r_ref, o_ref):
    o_ref[...] = r_ref[...] + jnp.dot(a_ref[...], w_ref[...], preferred_element_type=F32)


def matmul_residual(a, w, res):
    t, k = a.shape
    n = w.shape[1]
    tm = ROW_TILE
    nbytes = 2 * (tm * k * 2 + k * n * 2 + 2 * tm * n * 4)
    return pl.pallas_call(
        _matmul_residual_kernel,
        grid=(t // tm,),
        in_specs=[
            pl.BlockSpec((tm, k), lambda i: (i, 0)),
            pl.BlockSpec((k, n), lambda i: (0, 0)),
            pl.BlockSpec((tm, n), lambda i: (i, 0)),
        ],
        out_specs=pl.BlockSpec((tm, n), lambda i: (i, 0)),
        out_shape=jax.ShapeDtypeStruct((t, n), F32),
        compiler_params=_params(("parallel",), nbytes),
        name="matmul_residual",
    )(a, w, res)


def _split3(x):
    hi = x.astype(BF16)
    r1 = x - hi.astype(F32)
    mid = r1.astype(BF16)
    lo = (r1 - mid.astype(F32)).astype(BF16)
    return hi, mid, lo


def _gate_prep_kernel(g_ref, b_ref, tri_ref, ig_ref, bc_ref):
    z = g_ref[...] + b_ref[...]
    z = GATE_SOFTCAP * jnp.tanh(z / GATE_SOFTCAP)
    ig_ref[...] = z
    lf = jax.nn.log_sigmoid(z)
    hi, mid, lo = _split3(lf)
    parts = jnp.dot(tri_ref[...], jnp.concatenate([hi, mid, lo], axis=1), preferred_element_type=F32)
    w = V7X_LANES
    bc_ref[...] = (parts[:, :w] + parts[:, w:2 * w]) + parts[:, 2 * w:]


def gate_prep(graw, bias):
    t, w = graw.shape
    tl = MLSTM_TILE
    r = lax.broadcasted_iota(jnp.int32, (tl, tl), 0)
    c = lax.broadcasted_iota(jnp.int32, (tl, tl), 1)
    tri = (c <= r).astype(BF16)
    spec = pl.BlockSpec((tl, w), lambda i: (i, 0))
    return pl.pallas_call(
        _gate_prep_kernel,
        grid=(t // tl,),
        in_specs=[spec, pl.BlockSpec((1, w), lambda i: (0, 0)), pl.BlockSpec((tl, tl), lambda i: (0, 0))],
        out_specs=[spec, spec],
        out_shape=[jax.ShapeDtypeStruct((t, w), F32)] * 2,
        compiler_params=_params(("parallel",), 8 * tl * w * 4),
        name="gate_prep",
    )(graw, bias, tri)


def _mlstm_kernel(q_ref, k_ref, v_ref, og_ref, ig_ref, bc_ref, gh_ref, o_ref, state_ref):
    head = pl.program_id(1)
    tl = q_ref.shape[0]
    dv = v_ref.shape[1]

    @pl.when(pl.program_id(2) == 0)
    def _():
        state_ref[...] = jnp.zeros_like(state_ref)

    lane = lax.broadcasted_iota(jnp.int32, (tl, V7X_LANES), 1)
    mine = (lane // MLSTM_QK_DIM) == (head % 2)
    zero = jnp.zeros((), BF16)
    q = jnp.where(mine, q_ref[...], zero) * jnp.asarray(MLSTM_QK_DIM ** -0.5, BF16)
    k = jnp.where(mine, k_ref[...], zero)
    v = v_ref[...]

    ig = jnp.sum(jnp.where(lane == head, ig_ref[...], 0.0), axis=-1, keepdims=True)
    bc = jnp.sum(jnp.where(lane == head + MLSTM_HEADS, bc_ref[...], 0.0), axis=-1, keepdims=True)
    b_last = bc[tl - 1:tl, :]

    src = jnp.broadcast_to(bc - ig, (tl, V7X_LANES)).T[0:1, :]
    row = lax.broadcasted_iota(jnp.int32, (tl, tl), 0)
    col = lax.broadcasted_iota(jnp.int32, (tl, tl), 1)
    decay = jnp.exp(jnp.where(col <= row, bc - src, -jnp.inf))

    s = lax.dot_general(q, k, (((1,), (1,)), ((), ())), preferred_element_type=F32)
    p = (s * decay).astype(BF16)
    v_aug = jnp.concatenate([v, jnp.ones((tl, dv), BF16)], axis=1)
    st = state_ref[...]
    r = jnp.dot(p, v_aug, preferred_element_type=F32)
    r = r + jnp.exp(bc) * jnp.dot(q, st.astype(BF16), preferred_element_type=F32)
    num = r[:, :dv]
    den = r[:, dv:]
    h = num / jnp.maximum(jnp.abs(den), 1.0)
    h = h * lax.rsqrt(jnp.mean(h * h, axis=-1, keepdims=True) + EPS)
    h = h * gh_ref[...] * jax.nn.sigmoid(og_ref[...].astype(F32))
    o_ref[...] = h.astype(o_ref.dtype)

    w = jnp.exp(b_last - bc + ig)
    vw = jnp.concatenate([v.astype(F32) * w, jnp.broadcast_to(w, (tl, dv))], axis=1).astype(BF16)
    upd = lax.dot_general(k, vw, (((0,), (0,)), ((), ())), preferred_element_type=F32)
    state_ref[...] = jnp.exp(b_last) * st + upd


def mlstm_core(proj, igs, bcum, g_h, batch, seq):
    t = proj.shape[0]
    tl = MLSTM_TILE
    nt = seq // tl
    heads = MLSTM_HEADS
    dv = MLSTM_V_DIM
    w = V7X_LANES
    q_blocks = heads * MLSTM_QK_DIM // w
    v_block0 = 2 * q_blocks
    o_block0 = v_block0 + heads

    def rows(b, h, i):
        return b * nt + i

    nbytes = 2 * (4 * tl * w * 2 + 2 * tl * w * 4 + tl * dv * 2) + w * 2 * dv * 4 + 8 * tl * tl * 4
    return pl.pallas_call(
        _mlstm_kernel,
        grid=(batch, heads, nt),
        in_specs=[
            pl.BlockSpec((tl, w), lambda b, h, i: (rows(b, h, i), h // 2)),
            pl.BlockSpec((tl, w), lambda b, h, i: (rows(b, h, i), q_blocks + h // 2)),
            pl.BlockSpec((tl, dv), lambda b, h, i: (rows(b, h, i), v_block0 + h)),
            pl.BlockSpec((tl, dv), lambda b, h, i: (rows(b, h, i), o_block0 + h)),
            pl.BlockSpec((tl, w), lambda b, h, i: (rows(b, h, i), 0)),
            pl.BlockSpec((tl, w), lambda b, h, i: (rows(b, h, i), 0)),
            pl.BlockSpec((1, dv), lambda b, h, i: (0, h)),
        ],
        out_specs=pl.BlockSpec((tl, dv), lambda b, h, i: (rows(b, h, i), h)),
        out_shape=jax.ShapeDtypeStruct((t, heads * dv), BF16),
        scratch_shapes=[pltpu.VMEM((w, 2 * dv), F32)],
        compiler_params=_params(("parallel", "parallel", "arbitrary"), nbytes),
        name="mlstm_core",
    )(proj, proj, proj, proj, igs, bcum, g_h.reshape(1, heads * dv))


def _key_proj_kernel(x_ref, g_ref, w_ref, gk_ref, o_ref, xn_ref):
    @pl.when(pl.program_id(1) == 0)
    def _():
        xn_ref[...] = _rms_scale(x_ref[...], g_ref[...]).astype(BF16)

    y = jnp.dot(xn_ref[...], w_ref[...], preferred_element_type=F32)
    d = SB_HEAD_DIM
    for c in range(y.shape[1] // V7X_LANES):
        yt = y[:, c * V7X_LANES:(c + 1) * V7X_LANES].T
        for half in range(V7X_LANES // d):
            blk = yt[half * d:(half + 1) * d, :]
            ms = jnp.mean(blk * blk, axis=0, keepdims=True)
            kn = blk * lax.rsqrt(ms + EPS) * gk_ref[...]
            o_ref[0, c, half * d:(half + 1) * d, :] = kn.astype(o_ref.dtype)


def key_proj(x, g, w, g_k, batch, seq):
    t, d = x.shape
    n = w.shape[1]
    tm = ROW_TILE
    tn = 512
    ns = seq // tm
    pairs = n // V7X_LANES
    nbytes = 2 * (tm * d * 4 + d * tn * 2 + tm * tn * 2) + tm * d * 2 + 2 * tm * tn * 4
    return pl.pallas_call(
        _key_proj_kernel,
        grid=(t // tm, n // tn),
        in_specs=[
            pl.BlockSpec((tm, d), lambda i, j: (i, 0)),
            pl.BlockSpec((1, d), lambda i, j: (0, 0)),
            pl.BlockSpec((d, tn), lambda i, j: (0, j)),
            pl.BlockSpec((SB_HEAD_DIM, 1), lambda i, j: (0, 0)),
        ],
        out_specs=pl.BlockSpec((1, tn // V7X_LANES, V7X_LANES, tm), lambda i, j: (i // ns, j, 0, i % ns)),
        out_shape=jax.ShapeDtypeStruct((batch, pairs, V7X_LANES, seq), BF16),
        scratch_shapes=[pltpu.VMEM((tm, d), BF16)],
        compiler_params=_params(("parallel", "arbitrary"), nbytes),
        name="key_proj",
    )(x, g.reshape(1, d), w, g_k.reshape(SB_HEAD_DIM, 1))


LOG2E = 1.4426950408889634


def _sb_block(qh, kt, vb, upper, run, first_valid_col):
    tq, tk = qh.shape[0], kt.shape[1]
    z = jnp.dot(qh, kt, preferred_element_type=F32)
    sp = jnp.maximum(z, 0.0) + jnp.log(1.0 + jnp.exp2(-jnp.abs(z))) * LOG2E
    if first_valid_col is not None:
        valid = lax.broadcasted_iota(jnp.int32, (tq, tk), 1) < first_valid_col
        sp = jnp.where(valid, sp, 0.0)
    hi = sp.astype(BF16)
    lo = (sp - hi.astype(F32)).astype(BF16)
    both = jnp.dot(jnp.concatenate([hi, lo], axis=0), upper, preferred_element_type=F32)
    between = both[:tq] + both[tq:]
    a = jnp.exp2(((z - sp) - between) - run)
    if first_valid_col is not None:
        a = jnp.where(valid, a, 0.0)
    out = jnp.dot(a.astype(BF16), vb, preferred_element_type=F32)
    return out, run + jnp.sum(sp, axis=-1, keepdims=True)


def _sb_kernel(q_ref, gq_ref, kt_ref, v_ref, up_ref, o_ref):
    i = pl.program_id(2)
    tq = q_ref.shape[0]
    tk = up_ref.shape[0]
    ratio = tq // tk
    d = SB_HEAD_DIM
    x = q_ref[...]
    lane = lax.broadcasted_iota(jnp.int32, x.shape, 1)
    first = lane < d
    x2 = x * x
    ss_a = jnp.sum(jnp.where(first, x2, 0.0), axis=-1, keepdims=True)
    ss_b = jnp.sum(jnp.where(first, 0.0, x2), axis=-1, keepdims=True)
    inv = jnp.where(first, lax.rsqrt(ss_a / d + EPS), lax.rsqrt(ss_b / d + EPS))
    qn = x * inv * gq_ref[...] * (LOG2E * d ** -0.5)
    q_a = jnp.where(first, qn, 0.0).astype(BF16)
    q_b = jnp.where(first, 0.0, qn).astype(BF16)
    upper = up_ref[...]
    row = lax.broadcasted_iota(jnp.int32, (tq, 1), 0)

    def both_heads(kb, runs, first_valid_col):
        start = pl.multiple_of(kb * tk, tk)
        kt = kt_ref[0, 0, :, pl.ds(start, tk)]
        vb = v_ref[pl.ds(start, tk), :]
        out_a, run_a = _sb_block(q_a, kt, vb, upper, runs[0], first_valid_col)
        out_b, run_b = _sb_block(q_b, kt, vb, upper, runs[1], first_valid_col)
        return (out_a, out_b), (run_a, run_b)

    zero = jnp.zeros((tq, 1), F32)
    accs, runs = (jnp.zeros((tq, V7X_LANES), F32),) * 2, (zero, zero)
    for r in range(ratio - 1, -1, -1):
        outs, runs = both_heads(i * ratio + r, runs, row - r * tk)
        accs = (accs[0] + outs[0], accs[1] + outs[1])

    def body(j, carry):
        accs, runs = carry
        outs, runs = both_heads(i * ratio - j, runs, None)
        return (accs[0] + outs[0], accs[1] + outs[1]), runs

    accs, _ = lax.fori_loop(1, i * ratio + 1, body, (accs, runs))
    o_ref[...] = jnp.where(first, accs[0], accs[1]).astype(o_ref.dtype)


def stick_breaking(qraw, g_q, kt, v, batch, seq):
    t, width = qraw.shape
    tq = SB_Q_TILE
    tk = SB_K_TILE
    nq = seq // tq
    w = V7X_LANES
    pairs = width // w
    r = lax.broadcasted_iota(jnp.int32, (tk, tk), 0)
    c = lax.broadcasted_iota(jnp.int32, (tk, tk), 1)
    upper = (r > c).astype(BF16)
    gq2 = jnp.tile(g_q.reshape(1, SB_HEAD_DIM), (1, w // SB_HEAD_DIM))
    nbytes = 2 * (tq * w * 4 + 2 * seq * w * 2 + tk * tk * 2 + tq * w * 2) + 16 * tq * tk * 4
    return pl.pallas_call(
        _sb_kernel,
        grid=(batch, pairs, nq),
        in_specs=[
            pl.BlockSpec((tq, w), lambda b, p, i: (b * nq + i, p)),
            pl.BlockSpec((1, w), lambda b, p, i: (0, 0)),
            pl.BlockSpec((1, 1, w, seq), lambda b, p, i: (b, p, 0, 0)),
            pl.BlockSpec((seq, w), lambda b, p, i: (b, p)),
            pl.BlockSpec((tk, tk), lambda b, p, i: (0, 0)),
        ],
        out_specs=pl.BlockSpec((tq, w), lambda b, p, i: (b * nq + i, p)),
        out_shape=jax.ShapeDtypeStruct((t, width), BF16),
        compiler_params=_params(("parallel", "parallel", "arbitrary"), nbytes),
        name="stick_breaking",
    )(qraw, gq2, kt, v, upper)


def _ffn_kernel(x_ref, g_ref, wg_ref, wu_ref, wd_ref, o_ref, xn_ref, acc_ref):
    f = pl.program_id(1)

    @pl.when(f == 0)
    def _():
        xn_ref[...] = _rms_scale(x_ref[...], g_ref[...]).astype(BF16)
        acc_ref[...] = jnp.zeros_like(acc_ref)

    xn = xn_ref[...]
    hg = jnp.dot(xn, wg_ref[...], preferred_element_type=F32)
    hu = jnp.dot(xn, wu_ref[...], preferred_element_type=F32)
    act = (jax.nn.silu(hg) * hu).astype(BF16)
    acc_ref[...] += jnp.dot(act, wd_ref[...], preferred_element_type=F32)

    @pl.when(f == pl.num_programs(1) - 1)
    def _():
        o_ref[...] = x_ref[...] + acc_ref[...]


def ffn_dense(x, g, wg, wu, wd):
    t, d = x.shape
    f = wg.shape[1]
    tm = FFN_ROW_TILE
    tf = FFN_COL_TILE
    nbytes = 2 * (2 * tm * d * 4 + 3 * d * tf * 2) + tm * d * 6 + 4 * tm * tf * 4
    return pl.pallas_call(
        _ffn_kernel,
        grid=(t // tm, f // tf),
        in_specs=[
            pl.BlockSpec((tm, d), lambda i, j: (i, 0)),
            pl.BlockSpec((1, d), lambda i, j: (0, 0)),
            pl.BlockSpec((d, tf), lambda i, j: (0, j)),
            pl.BlockSpec((d, tf), lambda i, j: (0, j)),
            pl.BlockSpec((tf, d), lambda i, j: (j, 0)),
        ],
        out_specs=pl.BlockSpec((tm, d), lambda i, j: (i, 0)),
        out_shape=jax.ShapeDtypeStruct((t, d), F32),
        scratch_shapes=[pltpu.VMEM((tm, d), BF16), pltpu.VMEM((tm, d), F32)],
        compiler_params=_params(("parallel", "arbitrary"), nbytes),
        name="ffn_dense",
    )(x, g.reshape(1, d), wg, wu, wd)


def _router_kernel(x_ref, g_ref, wr_ref, tri_ref, xn_ref, mi_ref, mf_ref, cnt_ref, run_ref):
    @pl.when(pl.program_id(0) == 0)
    def _():
        run_ref[...] = jnp.zeros_like(run_ref)

    xn = _rms_scale(x_ref[...], g_ref[...])
    xn_ref[...] = xn
    logits = jnp.dot(xn.astype(BF16), wr_ref[...], preferred_element_type=F32)
    lane = lax.broadcasted_iota(jnp.int32, logits.shape, 1)
    neg = jnp.asarray(-jnp.inf, F32)
    logits = jnp.where(lane < N_EXPERTS, logits, neg)
    m1 = jnp.max(logits, axis=-1, keepdims=True)
    i1 = jnp.min(jnp.where(logits == m1, lane, V7X_LANES), axis=-1, keepdims=True)
    rest = jnp.where(lane == i1, neg, logits)
    m2 = jnp.max(rest, axis=-1, keepdims=True)
    i2 = jnp.min(jnp.where(rest == m2, lane, V7X_LANES), axis=-1, keepdims=True)
    e2 = jnp.exp(m2 - m1)
    denom = 1.0 + e2
    mf_ref[...] = jnp.where(lane == 0, 1.0 / denom, jnp.where(lane == 1, e2 / denom, 0.0))

    chosen = jnp.logical_or(lane == i1, lane == i2)
    onehot = jnp.where(chosen, 1.0, 0.0)
    rank = jnp.dot(tri_ref[...], onehot.astype(BF16), preferred_element_type=F32) + run_ref[...]
    r1 = jnp.sum(jnp.where(lane == i1, rank, 0.0), axis=-1, keepdims=True).astype(jnp.int32)
    r2 = jnp.sum(jnp.where(lane == i2, rank, 0.0), axis=-1, keepdims=True).astype(jnp.int32)
    mi_ref[...] = jnp.where(lane == 0, i1, jnp.where(lane == 1, i2, jnp.where(lane == 2, r1, jnp.where(lane == 3, r2, 0))))
    run_ref[...] += jnp.sum(onehot, axis=0, keepdims=True)
    cnt_ref[...] = run_ref[...]


def router(x, g, w_router_padded):
    t, d = x.shape
    tm = ROW_TILE
    w = V7X_LANES
    r = lax.broadcasted_iota(jnp.int32, (tm, tm), 0)
    c = lax.broadcasted_iota(jnp.int32, (tm, tm), 1)
    tri = (c < r).astype(BF16)
    nbytes = 2 * (2 * tm * d * 4 + d * w * 2 + tm * tm * 2 + 2 * tm * w * 4) + 2 * tm * d * 4
    row_spec = pl.BlockSpec((tm, w), lambda i: (i, 0))
    return pl.pallas_call(
        _router_kernel,
        grid=(t // tm,),
        in_specs=[
            pl.BlockSpec((tm, d), lambda i: (i, 0)),
            pl.BlockSpec((1, d), lambda i: (0, 0)),
            pl.BlockSpec((d, w), lambda i: (0, 0)),
            pl.BlockSpec((tm, tm), lambda i: (0, 0)),
        ],
        out_specs=[pl.BlockSpec((tm, d), lambda i: (i, 0)), row_spec, row_spec, pl.BlockSpec((1, w), lambda i: (0, 0))],
        out_shape=[
            jax.ShapeDtypeStruct((t, d), F32),
            jax.ShapeDtypeStruct((t, w), jnp.int32),
            jax.ShapeDtypeStruct((t, w), F32),
            jax.ShapeDtypeStruct((1, w), F32),
        ],
        scratch_shapes=[pltpu.VMEM((1, w), F32)],
        compiler_params=_params(("arbitrary",), nbytes),
        name="router",
    )(x, g.reshape(1, d), w_router_padded, tri)


def _row_copy(src_hbm, src_row, dst, dst_row, sem):
    return pltpu.make_async_copy(src_hbm.at[pl.ds(src_row, 1)], dst.at[pl.ds(dst_row, 1)], sem)


def _dispatch_kernel(slots_ref, xn_hbm, zeros_hbm, xs_hbm, sem):
    del zeros_hbm
    td = slots_ref.shape[0] // 2
    base = pl.program_id(0) * td

    def body(r, carry):
        for k in range(2):
            _row_copy(xn_hbm, base + r, xs_hbm, slots_ref[2 * r + k], sem).start()
        return carry

    lax.fori_loop(0, td, body, 0, unroll=8)
    for k in range(2):
        pltpu.make_async_copy(xn_hbm.at[pl.ds(0, td)], xs_hbm.at[pl.ds(0, td)], sem).wait()


def dispatch(slots, xn, n_slots):
    t, d = xn.shape
    td = MOE_ROUTE_TILE
    zeros = jnp.zeros((n_slots, d), xn.dtype)
    return pl.pallas_call(
        _dispatch_kernel,
        grid=(t // td,),
        in_specs=[
            pl.BlockSpec((2 * td,), lambda i: (i,), memory_space=pltpu.SMEM),
            pl.BlockSpec(memory_space=pl.ANY),
            pl.BlockSpec(memory_space=pl.ANY),
        ],
        out_specs=pl.BlockSpec(memory_space=pl.ANY),
        out_shape=jax.ShapeDtypeStruct((n_slots, d), xn.dtype),
        scratch_shapes=[pltpu.SemaphoreType.DMA],
        input_output_aliases={2: 0},
        compiler_params=pltpu.CompilerParams(dimension_semantics=("arbitrary",)),
        name="moe_dispatch",
    )(slots, xn, zeros)


def _experts_kernel(te_ref, na_ref, xs_ref, wg_ref, wu_ref, wd_ref, ys_ref, xb_ref, acc_ref):
    del te_ref
    f = pl.program_id(1)
    active = pl.program_id(0) < na_ref[0]

    @pl.when(jnp.logical_and(jnp.logical_not(active), f == 0))
    def _():
        ys_ref[...] = jnp.zeros_like(ys_ref)

    @pl.when(active)
    def _():
        @pl.when(f == 0)
        def _():
            xb_ref[...] = xs_ref[...].astype(BF16)
            acc_ref[...] = jnp.zeros_like(acc_ref)

        xb = xb_ref[...]
        hg = jnp.dot(xb, wg_ref[0], preferred_element_type=F32)
        hu = jnp.dot(xb, wu_ref[0], preferred_element_type=F32)
        act = (jax.nn.silu(hg) * hu).astype(BF16)
        acc_ref[...] += jnp.dot(act, wd_ref[0], preferred_element_type=F32)

        @pl.when(f == pl.num_programs(1) - 1)
        def _():
            ys_ref[...] = acc_ref[...]


def experts(tile_expert, n_active, xs, wg, wu, wd):
    n_slots, d = xs.shape
    fe = wg.shape[2]
    tg = MOE_GROUP_TILE
    tf = FFN_COL_TILE
    nf = fe // tf
    n_tiles = n_slots // tg

    def tile(n, na):
        return jnp.minimum(n, na[0] - 1)

    def col(n, f, na):
        return jnp.where(n < na[0], f, nf - 1)

    nbytes = 2 * (2 * tg * d * 4 + 3 * d * tf * 2) + tg * d * 6 + 4 * tg * tf * 4
    return pl.pallas_call(
        _experts_kernel,
        grid_spec=pltpu.PrefetchScalarGridSpec(
            num_scalar_prefetch=2,
            grid=(n_tiles, nf),
            in_specs=[
                pl.BlockSpec((tg, d), lambda n, f, te, na: (tile(n, na), 0)),
                pl.BlockSpec((1, d, tf), lambda n, f, te, na: (te[tile(n, na)], 0, col(n, f, na))),
                pl.BlockSpec((1, d, tf), lambda n, f, te, na: (te[tile(n, na)], 0, col(n, f, na))),
                pl.BlockSpec((1, tf, d), lambda n, f, te, na: (te[tile(n, na)], col(n, f, na), 0)),
            ],
            out_specs=pl.BlockSpec((tg, d), lambda n, f, te, na: (n, 0)),
            scratch_shapes=[pltpu.VMEM((tg, d), BF16), pltpu.VMEM((tg, d), F32)],
        ),
        out_shape=jax.ShapeDtypeStruct((n_slots, d), F32),
        compiler_params=_params(("arbitrary", "arbitrary"), nbytes),
        name="moe_experts",
    )(tile_expert, n_active, xs, wg, wu, wd)


def _combine_kernel(slots_ref, x_ref, mf_ref, ys_hbm, o_ref, buf_ref, sem):
    tc = x_ref.shape[0]

    def body(r, carry):
        for k in range(2):
            _row_copy(ys_hbm, slots_ref[2 * r + k], buf_ref.at[k], r, sem).start()
        return carry

    lax.fori_loop(0, tc, body, 0, unroll=8)
    for k in range(2):
        pltpu.make_async_copy(ys_hbm.at[pl.ds(0, tc)], buf_ref.at[k], sem).wait()
    lane = lax.broadcasted_iota(jnp.int32, mf_ref.shape, 1)
    mf = mf_ref[...]
    w1 = jnp.sum(jnp.where(lane == 0, mf, 0.0), axis=-1, keepdims=True)
    w2 = jnp.sum(jnp.where(lane == 1, mf, 0.0), axis=-1, keepdims=True)
    o_ref[...] = x_ref[...] + (w1 * buf_ref[0] + w2 * buf_ref[1])


def combine(slots, x, meta_f, ys):
    t, d = x.shape
    tc = MOE_ROUTE_TILE
    w = meta_f.shape[1]
    nbytes = 2 * (2 * tc * d * 4 + tc * w * 4) + 2 * tc * d * 4
    return pl.pallas_call(
        _combine_kernel,
        grid=(t // tc,),
        in_specs=[
            pl.BlockSpec((2 * tc,), lambda i: (i,), memory_space=pltpu.SMEM),
            pl.BlockSpec((tc, d), lambda i: (i, 0)),
            pl.BlockSpec((tc, w), lambda i: (i, 0)),
            pl.BlockSpec(memory_space=pl.ANY),
        ],
        out_specs=pl.BlockSpec((tc, d), lambda i: (i, 0)),
        out_shape=jax.ShapeDtypeStruct((t, d), F32),
        scratch_shapes=[pltpu.VMEM((2, tc, d), F32), pltpu.SemaphoreType.DMA],
        compiler_params=_params(("arbitrary",), nbytes),
        name="moe_combine",
    )(slots, x, meta_f, ys)


def moe_routed(x, g, w_router, wg, wu, wd):
    t, d = x.shape
    w = V7X_LANES
    tg = MOE_GROUP_TILE
    n_tiles = 2 * t // tg + N_EXPERTS
    w_r = jnp.pad(w_router, ((0, 0), (0, w - N_EXPERTS))).astype(BF16)
    xn, meta_i, meta_f, counts = router(x, g, w_r)

    counts = counts[0, :N_EXPERTS].astype(jnp.int32)
    tiles_per = (counts + tg - 1) // tg
    tile_end = jnp.cumsum(tiles_per)
    group_start = (tile_end - tiles_per) * tg
    n_active = tile_end[-1:].astype(jnp.int32)
    tile_ids = jnp.arange(n_tiles, dtype=jnp.int32)
    tile_expert = jnp.minimum(jnp.sum(tile_ids[:, None] >= tile_end[None, :], axis=1), N_EXPERTS - 1).astype(jnp.int32)
    slots = (jnp.take(group_start, meta_i[:, 0:2]) + meta_i[:, 2:4]).reshape(-1).astype(jnp.int32)

    xs = dispatch(slots, xn, n_tiles * tg)
    ys = experts(tile_expert, n_active, xs, wg, wu, wd)
    return combine(slots, x, meta_f, ys)


def kernel(x, mix_norm, ffn_norm, mlstm_w_in, mlstm_b_igate, mlstm_b_fgate, mlstm_g_h, mlstm_w_out, kv_norm, w_kv, g_k, sb_w_q, sb_g_q, sb_w_o, ffn_w_gate, ffn_w_up, ffn_w_down, moe_w_router, moe_w_gate, moe_w_up, moe_w_down):
    batch, seq, d = x.shape
    xt = x.reshape(batch * seq, d)
    w = V7X_LANES
    n_main = 2 * MLSTM_HEADS * MLSTM_QK_DIM + 2 * MLSTM_HEADS * MLSTM_V_DIM

    w_in = mlstm_w_in[0]
    w_main = w_in[:, :n_main].astype(BF16)
    w_gate = jnp.pad(w_in[:, n_main:], ((0, 0), (0, w - 2 * MLSTM_HEADS))).astype(BF16)
    bias = jnp.pad(jnp.concatenate([mlstm_b_igate[0], mlstm_b_fgate[0]]), (0, w - 2 * MLSTM_HEADS)).reshape(1, w)
    proj = norm_matmul(xt, mix_norm[0], w_main, BF16, tn=512)
    graw = norm_matmul(xt, mix_norm[0], w_gate, F32, tn=w)
    igs, bcum = gate_prep(graw, bias)
    hmix = mlstm_core(proj, igs, bcum, mlstm_g_h[0], batch, seq)
    xt = matmul_residual(hmix, mlstm_w_out[0].astype(BF16), xt)

    xt = ffn_dense(xt, ffn_norm[0], ffn_w_gate[0].astype(BF16), ffn_w_up[0].astype(BF16), ffn_w_down[0].astype(BF16))

    sb_width = SB_HEADS * SB_HEAD_DIM
    kt = key_proj(xt, kv_norm, w_kv[:, :sb_width].astype(BF16), g_k, batch, seq)
    v = norm_matmul(xt, kv_norm, w_kv[:, sb_width:].astype(BF16), BF16, tn=512)
    qraw = norm_matmul(xt, mix_norm[1], sb_w_q[0].astype(BF16), F32, tn=512)
    o = stick_breaking(qraw, sb_g_q[0], kt, v, batch, seq)
    xt = matmul_residual(o, sb_w_o[0].astype(BF16), xt)

    xt = moe_routed(xt, ffn_norm[1], moe_w_router[0], moe_w_gate[0].astype(BF16), moe_w_up[0].astype(BF16), moe_w_down[0].astype(BF16))
    return xt.reshape(batch, seq, d)
```

```python
import functools

import jax
import jax.numpy as jnp
from jax import lax
from jax.experimental import pallas as pl
from jax.experimental.pallas import tpu as pltpu

F32 = jnp.float32
BF16 = jnp.bfloat16

EPS = 1e-6
GATE_SOFTCAP = 15.0
MLSTM_HEADS = 8
MLSTM_QK_DIM = 64
MLSTM_V_DIM = 128
SB_HEADS = 16
SB_HEAD_DIM = 64
N_EXPERTS = 8

V7X_LANES = 128
V7X_VMEM_BYTES = 64 * 1024 * 1024

ROW_TILE = 512
FFN_ROW_TILE = 1024
FFN_COL_TILE = 256
MOE_COL_TILE = 512
MLSTM_TILE = 256
MOE_GROUP_TILE = 1024
MOE_ROUTE_TILE = 256
SB_Q_TILE = 512
SB_K_TILE = 256


def _vmem_limit(nbytes):
    return int(min(max(2 * nbytes, 32 * 1024 * 1024), V7X_VMEM_BYTES - 8 * 1024 * 1024))


def _params(semantics, nbytes):
    return pltpu.CompilerParams(dimension_semantics=semantics, vmem_limit_bytes=_vmem_limit(nbytes))


def _rms_scale(x, g):
    ms = jnp.mean(x * x, axis=-1, keepdims=True)
    return x * lax.rsqrt(ms + EPS) * g


def _norm_matmul_kernel(x_ref, g_ref, w_ref, o_ref, xn_ref):
    @pl.when(pl.program_id(1) == 0)
    def _():
        xn_ref[...] = _rms_scale(x_ref[...], g_ref[...]).astype(BF16)

    o_ref[...] = jnp.dot(xn_ref[...], w_ref[...], preferred_element_type=F32).astype(o_ref.dtype)


def norm_matmul(x, g, w, out_dtype, tn):
    t, d = x.shape
    n = w.shape[1]
    tm = ROW_TILE
    nbytes = 2 * (tm * d * 4 + d * tn * 2 + tm * tn * 4) + tm * d * 2
    return pl.pallas_call(
        _norm_matmul_kernel,
        grid=(t // tm, n // tn),
        in_specs=[
            pl.BlockSpec((tm, d), lambda i, j: (i, 0)),
            pl.BlockSpec((1, d), lambda i, j: (0, 0)),
            pl.BlockSpec((d, tn), lambda i, j: (0, j)),
        ],
        out_specs=pl.BlockSpec((tm, tn), lambda i, j: (i, j)),
        out_shape=jax.ShapeDtypeStruct((t, n), out_dtype),
        scratch_shapes=[pltpu.VMEM((tm, d), BF16)],
        compiler_params=_params(("parallel", "arbitrary"), nbytes),
        name="norm_matmul",
    )(x, g.reshape(1, d), w)


def _matmul_residual_kernel(a_ref, w_ref, r_ref, o_ref):
    o_ref[...] = r_ref[...] + jnp.dot(a_ref[...], w_ref[...], preferred_element_type=F32)


def matmul_residual(a, w, res):
    t, k = a.shape
    n = w.shape[1]
    tm = ROW_TILE
    nbytes = 2 * (tm * k * 2 + k * n * 2 + 2 * tm * n * 4)
    return pl.pallas_call(
        _matmul_residual_kernel,
        grid=(t // tm,),
        in_specs=[
            pl.BlockSpec((tm, k), lambda i: (i, 0)),
            pl.BlockSpec((k, n), lambda i: (0, 0)),
            pl.BlockSpec((tm, n), lambda i: (i, 0)),
        ],
        out_specs=pl.BlockSpec((tm, n), lambda i: (i, 0)),
        out_shape=jax.ShapeDtypeStruct((t, n), F32),
        compiler_params=_params(("parallel",), nbytes),
        name="matmul_residual",
    )(a, w, res)


def _split3(x):
    hi = x.astype(BF16)
    r1 = x - hi.astype(F32)
    mid = r1.astype(BF16)
    lo = (r1 - mid.astype(F32)).astype(BF16)
    return hi, mid, lo


def _gate_prep_kernel(g_ref, b_ref, tri_ref, ig_ref, bc_ref):
    z = g_ref[...] + b_ref[...]
    z = GATE_SOFTCAP * jnp.tanh(z / GATE_SOFTCAP)
    ig_ref[...] = z
    lf = jax.nn.log_sigmoid(z)
    hi, mid, lo = _split3(lf)
    parts = jnp.dot(tri_ref[...], jnp.concatenate([hi, mid, lo], axis=1), preferred_element_type=F32)
    w = V7X_LANES
    bc_ref[...] = (parts[:, :w] + parts[:, w:2 * w]) + parts[:, 2 * w:]


def gate_prep(graw, bias):
    t, w = graw.shape
    tl = MLSTM_TILE
    r = lax.broadcasted_iota(jnp.int32, (tl, tl), 0)
    c = lax.broadcasted_iota(jnp.int32, (tl, tl), 1)
    tri = (c <= r).astype(BF16)
    spec = pl.BlockSpec((tl, w), lambda i: (i, 0))
    return pl.pallas_call(
        _gate_prep_kernel,
        grid=(t // tl,),
        in_specs=[spec, pl.BlockSpec((1, w), lambda i: (0, 0)), pl.BlockSpec((tl, tl), lambda i: (0, 0))],
        out_specs=[spec, spec],
        out_shape=[jax.ShapeDtypeStruct((t, w), F32)] * 2,
        compiler_params=_params(("parallel",), 8 * tl * w * 4),
        name="gate_prep",
    )(graw, bias, tri)


def _mlstm_head(head, q_pair, k_pair, v, og, ig, bc, gh, causal, state_ref):
    tl, dv = v.shape
    lane = lax.broadcasted_iota(jnp.int32, (tl, V7X_LANES), 1)
    mine = (lane // MLSTM_QK_DIM) == (head % 2)
    zero = jnp.zeros((), BF16)
    q = jnp.where(mine, q_pair, zero) * jnp.asarray(MLSTM_QK_DIM ** -0.5, BF16)
    k = jnp.where(mine, k_pair, zero)
    b_last = bc[tl - 1:tl, :]

    src = jnp.broadcast_to(bc - ig, (tl, V7X_LANES)).T[0:1, :]
    decay = jnp.exp(jnp.where(causal, bc - src, -jnp.inf))

    s = lax.dot_general(q, k, (((1,), (1,)), ((), ())), preferred_element_type=F32)
    p = (s * decay).astype(BF16)
    v_aug = jnp.concatenate([v, jnp.ones((tl, dv), BF16)], axis=1)
    st = state_ref[...]
    r = jnp.dot(p, v_aug, preferred_element_type=F32)
    r = r + jnp.exp(bc) * jnp.dot(q, st.astype(BF16), preferred_element_type=F32)
    num = r[:, :dv]
    den = r[:, dv:]
    h = num / jnp.maximum(jnp.abs(den), 1.0)
    h = h * lax.rsqrt(jnp.mean(h * h, axis=-1, keepdims=True) + EPS)
    h = h * gh * jax.nn.sigmoid(og.astype(F32))

    w = jnp.exp(b_last - bc + ig)
    vw = jnp.concatenate([v.astype(F32) * w, jnp.broadcast_to(w, (tl, dv))], axis=1).astype(BF16)
    upd = lax.dot_general(k, vw, (((0,), (0,)), ((), ())), preferred_element_type=F32)
    state_ref[...] = jnp.exp(b_last) * st + upd
    return h


def _mlstm_kernel(q_ref, k_ref, v_ref, og_ref, ig_ref, bc_ref, gh_ref, o_ref, state_ref):
    tl = q_ref.shape[0]
    dv = MLSTM_V_DIM
    w = V7X_LANES

    @pl.when(pl.program_id(1) == 0)
    def _():
        state_ref[...] = jnp.zeros_like(state_ref)

    row = lax.broadcasted_iota(jnp.int32, (tl, tl), 0)
    col = lax.broadcasted_iota(jnp.int32, (tl, tl), 1)
    causal = col <= row
    for head in range(MLSTM_HEADS):
        pair = slice((head // 2) * w, (head // 2 + 1) * w)
        mine = slice(head * dv, (head + 1) * dv)
        h = _mlstm_head(head, q_ref[:, pair], k_ref[:, pair], v_ref[:, mine], og_ref[:, mine],
                        ig_ref[:, head:head + 1], bc_ref[:, MLSTM_HEADS + head:MLSTM_HEADS + head + 1],
                        gh_ref[:, mine], causal, state_ref.at[head])
        o_ref[:, mine] = h.astype(o_ref.dtype)


def mlstm_core(proj, igs, bcum, g_h, batch, seq):
    t = proj.shape[0]
    tl = MLSTM_TILE
    nt = seq // tl
    heads = MLSTM_HEADS
    dv = MLSTM_V_DIM
    w = V7X_LANES
    qk = heads * MLSTM_QK_DIM
    hv = heads * dv

    def rows(b, i):
        return b * nt + i

    nbytes = 2 * (2 * tl * qk * 2 + 3 * tl * hv * 2 + 2 * tl * w * 4) + heads * w * 2 * dv * 4 + 16 * tl * tl * 4
    return pl.pallas_call(
        _mlstm_kernel,
        grid=(batch, nt),
        in_specs=[
            pl.BlockSpec((tl, qk), lambda b, i: (rows(b, i), 0)),
            pl.BlockSpec((tl, qk), lambda b, i: (rows(b, i), 1)),
            pl.BlockSpec((tl, hv), lambda b, i: (rows(b, i), 2 * qk // hv)),
            pl.BlockSpec((tl, hv), lambda b, i: (rows(b, i), 2 * qk // hv + 1)),
            pl.BlockSpec((tl, w), lambda b, i: (rows(b, i), 0)),
            pl.BlockSpec((tl, w), lambda b, i: (rows(b, i), 0)),
            pl.BlockSpec((1, hv), lambda b, i: (0, 0)),
        ],
        out_specs=pl.BlockSpec((tl, hv), lambda b, i: (rows(b, i), 0)),
        out_shape=jax.ShapeDtypeStruct((t, hv), BF16),
        scratch_shapes=[pltpu.VMEM((heads, w, 2 * dv), F32)],
        compiler_params=_params(("parallel", "arbitrary"), nbytes),
        name="mlstm_core",
    )(proj, proj, proj, proj, igs, bcum, g_h.reshape(1, hv))


def _key_proj_kernel(x_ref, g_ref, w_ref, gk_ref, o_ref, xn_ref):
    @pl.when(pl.program_id(1) == 0)
    def _():
        xn_ref[...] = _rms_scale(x_ref[...], g_ref[...]).astype(BF16)

    y = jnp.dot(xn_ref[...], w_ref[...], preferred_element_type=F32)
    d = SB_HEAD_DIM
    for c in range(y.shape[1] // V7X_LANES):
        yt = y[:, c * V7X_LANES:(c + 1) * V7X_LANES].T
        for half in range(V7X_LANES // d):
            blk = yt[half * d:(half + 1) * d, :]
            ms = jnp.mean(blk * blk, axis=0, keepdims=True)
            kn = blk * lax.rsqrt(ms + EPS) * gk_ref[...]
            o_ref[0, c, half * d:(half + 1) * d, :] = kn.astype(o_ref.dtype)


def key_proj(x, g, w, g_k, batch, seq):
    t, d = x.shape
    n = w.shape[1]
    tm = ROW_TILE
    tn = 512
    ns = seq // tm
    pairs = n // V7X_LANES
    nbytes = 2 * (tm * d * 4 + d * tn * 2 + tm * tn * 2) + tm * d * 2 + 2 * tm * tn * 4
    return pl.pallas_call(
        _key_proj_kernel,
        grid=(t // tm, n // tn),
        in_specs=[
            pl.BlockSpec((tm, d), lambda i, j: (i, 0)),
            pl.BlockSpec((1, d), lambda i, j: (0, 0)),
            pl.BlockSpec((d, tn), lambda i, j: (0, j)),
            pl.BlockSpec((SB_HEAD_DIM, 1), lambda i, j: (0, 0)),
        ],
        out_specs=pl.BlockSpec((1, tn // V7X_LANES, V7X_LANES, tm), lambda i, j: (i // ns, j, 0, i % ns)),
        out_shape=jax.ShapeDtypeStruct((batch, pairs, V7X_LANES, seq), BF16),
        scratch_shapes=[pltpu.VMEM((tm, d), BF16)],
        compiler_params=_params(("parallel", "arbitrary"), nbytes),
        name="key_proj",
    )(x, g.reshape(1, d), w, g_k.reshape(SB_HEAD_DIM, 1))


LOG2E = 1.4426950408889634


def _sb_scores(qh, kt, z_ref):
    z_ref[...] = jnp.dot(qh, kt, preferred_element_type=F32)


def _sb_sticks(z_ref, upper, run, first_valid_col, zs_ref, bt_ref):
    tq, tk = z_ref.shape
    z = z_ref[...]
    sp = jnp.maximum(z, 0.0) + jnp.log(1.0 + jnp.exp2(-jnp.abs(z))) * LOG2E
    zs = (z - sp) - run
    if first_valid_col is not None:
        valid = lax.broadcasted_iota(jnp.int32, (tq, tk), 1) < first_valid_col
        sp = jnp.where(valid, sp, 0.0)
        zs = jnp.where(valid, zs, -jnp.inf)
    zs_ref[...] = zs
    hi = sp.astype(BF16)
    lo = (sp - hi.astype(F32)).astype(BF16)
    both = jnp.dot(jnp.concatenate([hi, lo], axis=0), upper, preferred_element_type=F32)
    bt_ref[...] = both[:tq] + both[tq:]
    return run + jnp.sum(sp, axis=-1, keepdims=True)


def _sb_values(zs_ref, bt_ref, vb, acc_ref):
    a = jnp.exp2(zs_ref[...] - bt_ref[...])
    acc_ref[...] += jnp.dot(a.astype(BF16), vb, preferred_element_type=F32)


def _sb_kernel(q_ref, gq_ref, kt_ref, v_ref, up_ref, o_ref, z_ref, zs_ref, bt_ref, acc_ref):
    i = pl.program_id(2)
    tq = q_ref.shape[0]
    tk = up_ref.shape[0]
    assert tq == 2 * tk, "the schedule below visits the key blocks of a query tile in pairs"
    d = SB_HEAD_DIM
    x = q_ref[...]
    lane = lax.broadcasted_iota(jnp.int32, x.shape, 1)
    first = lane < d
    x2 = x * x
    ss_a = jnp.sum(jnp.where(first, x2, 0.0), axis=-1, keepdims=True)
    ss_b = jnp.sum(jnp.where(first, 0.0, x2), axis=-1, keepdims=True)
    inv = jnp.where(first, lax.rsqrt(ss_a / d + EPS), lax.rsqrt(ss_b / d + EPS))
    qn = x * inv * gq_ref[...] * (LOG2E * d ** -0.5)
    heads = (jnp.where(first, qn, 0.0).astype(BF16), jnp.where(first, 0.0, qn).astype(BF16))
    upper = up_ref[...]
    row = lax.broadcasted_iota(jnp.int32, (tq, 1), 0)
    n_blocks = 2 * i + 2

    def start(n):
        kb = jnp.maximum(n_blocks - 1 - n, 0)
        return pl.multiple_of(kb * tk, tk)

    def scores(n, slot):
        kt = kt_ref[0, 0, :, pl.ds(start(n), tk)]
        for h in range(2):
            _sb_scores(heads[h], kt, z_ref.at[slot, h])

    def sticks(slot, runs, first_valid_col):
        return tuple(_sb_sticks(z_ref.at[slot, h], upper, runs[h], first_valid_col, zs_ref.at[slot, h],
                                bt_ref.at[slot, h]) for h in range(2))

    def values(n, slot):
        vb = v_ref[pl.ds(start(n), tk), :]
        for h in range(2):
            _sb_values(zs_ref.at[slot, h], bt_ref.at[slot, h], vb, acc_ref.at[h])

    acc_ref[...] = jnp.zeros_like(acc_ref)
    zero = jnp.zeros((tq, 1), F32)
    scores(0, 0)
    scores(1, 1)
    runs = sticks(0, (zero, zero), row - tk)
    scores(2, 0)
    runs = sticks(1, runs, row)
    values(0, 0)

    def body(m, runs):
        n = 2 + 2 * m
        scores(n + 1, 1)
        runs = sticks(0, runs, None)
        values(n - 1, 1)
        scores(n + 2, 0)
        runs = sticks(1, runs, None)
        values(n, 0)
        return runs

    lax.fori_loop(0, i, body, runs)
    values(n_blocks - 1, 1)
    o_ref[...] = jnp.where(first, acc_ref[0], acc_ref[1]).astype(o_ref.dtype)


def stick_breaking(qraw, g_q, kt, v, batch, seq):
    t, width = qraw.shape
    tq = SB_Q_TILE
    tk = SB_K_TILE
    nq = seq // tq
    w = V7X_LANES
    pairs = width // w
    r = lax.broadcasted_iota(jnp.int32, (tk, tk), 0)
    c = lax.broadcasted_iota(jnp.int32, (tk, tk), 1)
    upper = (r > c).astype(BF16)
    gq2 = jnp.tile(g_q.reshape(1, SB_HEAD_DIM), (1, w // SB_HEAD_DIM))
    nbytes = 2 * (tq * w * 4 + 2 * seq * w * 2 + tk * tk * 2 + tq * w * 2) + 32 * tq * tk * 4
    return pl.pallas_call(
        _sb_kernel,
        grid=(batch, pairs, nq),
        in_specs=[
            pl.BlockSpec((tq, w), lambda b, p, i: (b * nq + i, p)),
            pl.BlockSpec((1, w), lambda b, p, i: (0, 0)),
            pl.BlockSpec((1, 1, w, seq), lambda b, p, i: (b, p, 0, 0)),
            pl.BlockSpec((seq, w), lambda b, p, i: (b, p)),
            pl.BlockSpec((tk, tk), lambda b, p, i: (0, 0)),
        ],
        out_specs=pl.BlockSpec((tq, w), lambda b, p, i: (b * nq + i, p)),
        out_shape=jax.ShapeDtypeStruct((t, width), BF16),
        scratch_shapes=[
            pltpu.VMEM((2, 2, tq, tk), F32),
            pltpu.VMEM((2, 2, tq, tk), F32),
            pltpu.VMEM((2, 2, tq, tk), F32),
            pltpu.VMEM((2, tq, w), F32),
        ],
        compiler_params=_params(("parallel", "parallel", "arbitrary"), nbytes),
        name="stick_breaking",
    )(qraw, gq2, kt, v, upper)


def _ffn_kernel(x_ref, g_ref, wg_ref, wu_ref, wd_ref, o_ref, xn_ref, acc_ref):
    f = pl.program_id(1)

    @pl.when(f == 0)
    def _():
        xn_ref[...] = _rms_scale(x_ref[...], g_ref[...]).astype(BF16)
        acc_ref[...] = jnp.zeros_like(acc_ref)

    xn = xn_ref[...]
    hg = jnp.dot(xn, wg_ref[...], preferred_element_type=F32)
    hu = jnp.dot(xn, wu_ref[...], preferred_element_type=F32)
    act = (jax.nn.silu(hg) * hu).astype(BF16)
    acc_ref[...] += jnp.dot(act, wd_ref[...], preferred_element_type=F32)

    @pl.when(f == pl.num_programs(1) - 1)
    def _():
        o_ref[...] = x_ref[...] + acc_ref[...]


def ffn_dense(x, g, wg, wu, wd):
    t, d = x.shape
    f = wg.shape[1]
    tm = FFN_ROW_TILE
    tf = FFN_COL_TILE
    nbytes = 2 * (2 * tm * d * 4 + 3 * d * tf * 2) + tm * d * 6 + 4 * tm * tf * 4
    return pl.pallas_call(
        _ffn_kernel,
        grid=(t // tm, f // tf),
        in_specs=[
            pl.BlockSpec((tm, d), lambda i, j: (i, 0)),
            pl.BlockSpec((1, d), lambda i, j: (0, 0)),
            pl.BlockSpec((d, tf), lambda i, j: (0, j)),
            pl.BlockSpec((d, tf), lambda i, j: (0, j)),
            pl.BlockSpec((tf, d), lambda i, j: (j, 0)),
        ],
        out_specs=pl.BlockSpec((tm, d), lambda i, j: (i, 0)),
        out_shape=jax.ShapeDtypeStruct((t, d), F32),
        scratch_shapes=[pltpu.VMEM((tm, d), BF16), pltpu.VMEM((tm, d), F32)],
        compiler_params=_params(("parallel", "arbitrary"), nbytes),
        name="ffn_dense",
    )(x, g.reshape(1, d), wg, wu, wd)


def _router_kernel(x_ref, g_ref, wr_ref, tri_ref, xn_ref, mi_ref, mf_ref, cnt_ref, run_ref):
    @pl.when(pl.program_id(0) == 0)
    def _():
        run_ref[...] = jnp.zeros_like(run_ref)

    xn = _rms_scale(x_ref[...], g_ref[...])
    xn_ref[...] = xn
    logits = jnp.dot(xn.astype(BF16), wr_ref[...], preferred_element_type=F32)
    lane = lax.broadcasted_iota(jnp.int32, logits.shape, 1)
    neg = jnp.asarray(-jnp.inf, F32)
    logits = jnp.where(lane < N_EXPERTS, logits, neg)
    m1 = jnp.max(logits, axis=-1, keepdims=True)
    i1 = jnp.min(jnp.where(logits == m1, lane, V7X_LANES), axis=-1, keepdims=True)
    rest = jnp.where(lane == i1, neg, logits)
    m2 = jnp.max(rest, axis=-1, keepdims=True)
    i2 = jnp.min(jnp.where(rest == m2, lane, V7X_LANES), axis=-1, keepdims=True)
    e2 = jnp.exp(m2 - m1)
    denom = 1.0 + e2
    mf_ref[...] = jnp.where(lane == 0, 1.0 / denom, jnp.where(lane == 1, e2 / denom, 0.0))

    chosen = jnp.logical_or(lane == i1, lane == i2)
    onehot = jnp.where(chosen, 1.0, 0.0)
    rank = jnp.dot(tri_ref[...], onehot.astype(BF16), preferred_element_type=F32) + run_ref[...]
    r1 = jnp.sum(jnp.where(lane == i1, rank, 0.0), axis=-1, keepdims=True).astype(jnp.int32)
    r2 = jnp.sum(jnp.where(lane == i2, rank, 0.0), axis=-1, keepdims=True).astype(jnp.int32)
    mi_ref[...] = jnp.where(lane == 0, i1, jnp.where(lane == 1, i2, jnp.where(lane == 2, r1, jnp.where(lane == 3, r2, 0))))
    run_ref[...] += jnp.sum(onehot, axis=0, keepdims=True)
    cnt_ref[...] = run_ref[...]


def router(x, g, w_router_padded):
    t, d = x.shape
    tm = ROW_TILE
    w = V7X_LANES
    r = lax.broadcasted_iota(jnp.int32, (tm, tm), 0)
    c = lax.broadcasted_iota(jnp.int32, (tm, tm), 1)
    tri = (c < r).astype(BF16)
    nbytes = 2 * (2 * tm * d * 4 + d * w * 2 + tm * tm * 2 + 2 * tm * w * 4) + 2 * tm * d * 4
    row_spec = pl.BlockSpec((tm, w), lambda i: (i, 0))
    return pl.pallas_call(
        _router_kernel,
        grid=(t // tm,),
        in_specs=[
            pl.BlockSpec((tm, d), lambda i: (i, 0)),
            pl.BlockSpec((1, d), lambda i: (0, 0)),
            pl.BlockSpec((d, w), lambda i: (0, 0)),
            pl.BlockSpec((tm, tm), lambda i: (0, 0)),
        ],
        out_specs=[pl.BlockSpec((tm, d), lambda i: (i, 0)), row_spec, row_spec, pl.BlockSpec((1, w), lambda i: (0, 0))],
        out_shape=[
            jax.ShapeDtypeStruct((t, d), F32),
            jax.ShapeDtypeStruct((t, w), jnp.int32),
            jax.ShapeDtypeStruct((t, w), F32),
            jax.ShapeDtypeStruct((1, w), F32),
        ],
        scratch_shapes=[pltpu.VMEM((1, w), F32)],
        compiler_params=_params(("arbitrary",), nbytes),
        name="router",
    )(x, g.reshape(1, d), w_router_padded, tri)


def _row_copy(src, src_row, dst, dst_row, sem):
    return pltpu.make_async_copy(src.at[pl.ds(src_row, 1)], dst.at[pl.ds(dst_row, 1)], sem)


def _dispatch_kernel(slots_ref, xn_ref, zeros_hbm, xs_hbm, sem):
    del zeros_hbm
    td = xn_ref.shape[0]

    def body(r, carry):
        for k in range(2):
            _row_copy(xn_ref, r, xs_hbm, slots_ref[2 * r + k], sem).start()
        return carry

    lax.fori_loop(0, td, body, 0, unroll=8)
    for k in range(2):
        pltpu.make_async_copy(xn_ref, xs_hbm.at[pl.ds(0, td)], sem).wait()


def dispatch(slots, xn, n_slots):
    t, d = xn.shape
    td = MOE_ROUTE_TILE
    zeros = jnp.zeros((n_slots, d), xn.dtype)
    return pl.pallas_call(
        _dispatch_kernel,
        grid=(t // td,),
        in_specs=[
            pl.BlockSpec((2 * td,), lambda i: (i,), memory_space=pltpu.SMEM),
            pl.BlockSpec((td, d), lambda i: (i, 0)),
            pl.BlockSpec(memory_space=pl.ANY),
        ],
        out_specs=pl.BlockSpec(memory_space=pl.ANY),
        out_shape=jax.ShapeDtypeStruct((n_slots, d), xn.dtype),
        scratch_shapes=[pltpu.SemaphoreType.DMA],
        input_output_aliases={2: 0},
        compiler_params=_params(("arbitrary",), 2 * td * d * 4),
        name="moe_dispatch",
    )(slots, xn, zeros)


def _experts_kernel(te_ref, na_ref, xs_ref, wg_ref, wu_ref, wd_ref, ys_ref, xb_ref, acc_ref):
    del te_ref
    f = pl.program_id(1)
    active = pl.program_id(0) < na_ref[0]

    @pl.when(jnp.logical_and(jnp.logical_not(active), f == 0))
    def _():
        ys_ref[...] = jnp.zeros_like(ys_ref)

    @pl.when(active)
    def _():
        @pl.when(f == 0)
        def _():
            xb_ref[...] = xs_ref[...].astype(BF16)
            acc_ref[...] = jnp.zeros_like(acc_ref)

        xb = xb_ref[...]
        hg = jnp.dot(xb, wg_ref[0], preferred_element_type=F32)
        hu = jnp.dot(xb, wu_ref[0], preferred_element_type=F32)
        act = (jax.nn.silu(hg) * hu).astype(BF16)
        acc_ref[...] += jnp.dot(act, wd_ref[0], preferred_element_type=F32)

        @pl.when(f == pl.num_programs(1) - 1)
        def _():
            ys_ref[...] = acc_ref[...]


def experts(tile_expert, n_active, xs, wg, wu, wd):
    n_slots, d = xs.shape
    fe = wg.shape[2]
    tg = MOE_GROUP_TILE
    tf = MOE_COL_TILE
    nf = fe // tf
    n_tiles = n_slots // tg

    def tile(n, na):
        return jnp.minimum(n, na[0] - 1)

    def col(n, f, na):
        return jnp.where(n < na[0], f, nf - 1)

    nbytes = 2 * (2 * tg * d * 4 + 3 * d * tf * 2) + tg * d * 6 + 4 * tg * tf * 4
    return pl.pallas_call(
        _experts_kernel,
        grid_spec=pltpu.PrefetchScalarGridSpec(
            num_scalar_prefetch=2,
            grid=(n_tiles, nf),
            in_specs=[
                pl.BlockSpec((tg, d), lambda n, f, te, na: (tile(n, na), 0)),
                pl.BlockSpec((1, d, tf), lambda n, f, te, na: (te[tile(n, na)], 0, col(n, f, na))),
                pl.BlockSpec((1, d, tf), lambda n, f, te, na: (te[tile(n, na)], 0, col(n, f, na))),
                pl.BlockSpec((1, tf, d), lambda n, f, te, na: (te[tile(n, na)], col(n, f, na), 0)),
            ],
            out_specs=pl.BlockSpec((tg, d), lambda n, f, te, na: (n, 0)),
            scratch_shapes=[pltpu.VMEM((tg, d), BF16), pltpu.VMEM((tg, d), F32)],
        ),
        out_shape=jax.ShapeDtypeStruct((n_slots, d), F32),
        compiler_params=_params(("arbitrary", "arbitrary"), nbytes),
        name="moe_experts",
    )(tile_expert, n_active, xs, wg, wu, wd)


def _combine_kernel(slots_ref, x_ref, mf_ref, ys_hbm, o_ref, buf_ref, sem):
    tc = x_ref.shape[0]

    def body(r, carry):
        for k in range(2):
            _row_copy(ys_hbm, slots_ref[2 * r + k], buf_ref.at[k], r, sem).start()
        return carry

    lax.fori_loop(0, tc, body, 0, unroll=8)
    for k in range(2):
        pltpu.make_async_copy(ys_hbm.at[pl.ds(0, tc)], buf_ref.at[k], sem).wait()
    lane = lax.broadcasted_iota(jnp.int32, mf_ref.shape, 1)
    mf = mf_ref[...]
    w1 = jnp.sum(jnp.where(lane == 0, mf, 0.0), axis=-1, keepdims=True)
    w2 = jnp.sum(jnp.where(lane == 1, mf, 0.0), axis=-1, keepdims=True)
    o_ref[...] = x_ref[...] + (w1 * buf_ref[0] + w2 * buf_ref[1])


def combine(slots, x, meta_f, ys):
    t, d = x.shape
    tc = MOE_ROUTE_TILE
    w = meta_f.shape[1]
    nbytes = 2 * (2 * tc * d * 4 + tc * w * 4) + 2 * tc * d * 4
    return pl.pallas_call(
        _combine_kernel,
        grid=(t // tc,),
        in_specs=[
            pl.BlockSpec((2 * tc,), lambda i: (i,), memory_space=pltpu.SMEM),
            pl.BlockSpec((tc, d), lambda i: (i, 0)),
            pl.BlockSpec((tc, w), lambda i: (i, 0)),
            pl.BlockSpec(memory_space=pl.ANY),
        ],
        out_specs=pl.BlockSpec((tc, d), lambda i: (i, 0)),
        out_shape=jax.ShapeDtypeStruct((t, d), F32),
        scratch_shapes=[pltpu.VMEM((2, tc, d), F32), pltpu.SemaphoreType.DMA],
        compiler_params=_params(("arbitrary",), nbytes),
        name="moe_combine",
    )(slots, x, meta_f, ys)


def moe_routed(x, g, w_router, wg, wu, wd):
    t, d = x.shape
    w = V7X_LANES
    tg = MOE_GROUP_TILE
    n_tiles = 2 * t // tg + N_EXPERTS
    w_r = jnp.pad(w_router, ((0, 0), (0, w - N_EXPERTS))).astype(BF16)
    xn, meta_i, meta_f, counts = router(x, g, w_r)

    counts = counts[0, :N_EXPERTS].astype(jnp.int32)
    tiles_per = (counts + tg - 1) // tg
    tile_end = jnp.cumsum(tiles_per)
    group_start = (tile_end - tiles_per) * tg
    n_active = tile_end[-1:].astype(jnp.int32)
    tile_ids = jnp.arange(n_tiles, dtype=jnp.int32)
    tile_expert = jnp.minimum(jnp.sum(tile_ids[:, None] >= tile_end[None, :], axis=1), N_EXPERTS - 1).astype(jnp.int32)
    slots = (jnp.take(group_start, meta_i[:, 0:2]) + meta_i[:, 2:4]).reshape(-1).astype(jnp.int32)

    xs = dispatch(slots, xn, n_tiles * tg)
    ys = experts(tile_expert, n_active, xs, wg, wu, wd)
    return combine(slots, x, meta_f, ys)


def kernel(x, mix_norm, ffn_norm, mlstm_w_in, mlstm_b_igate, mlstm_b_fgate, mlstm_g_h, mlstm_w_out, kv_norm, w_kv, g_k, sb_w_q, sb_g_q, sb_w_o, ffn_w_gate, ffn_w_up, ffn_w_down, moe_w_router, moe_w_gate, moe_w_up, moe_w_down):
    batch, seq, d = x.shape
    xt = x.reshape(batch * seq, d)
    w = V7X_LANES
    n_main = 2 * MLSTM_HEADS * MLSTM_QK_DIM + 2 * MLSTM_HEADS * MLSTM_V_DIM

    w_in = mlstm_w_in[0]
    w_main = w_in[:, :n_main].astype(BF16)
    w_gate = jnp.pad(w_in[:, n_main:], ((0, 0), (0, w - 2 * MLSTM_HEADS))).astype(BF16)
    bias = jnp.pad(jnp.concatenate([mlstm_b_igate[0], mlstm_b_fgate[0]]), (0, w - 2 * MLSTM_HEADS)).reshape(1, w)
    proj = norm_matmul(xt, mix_norm[0], w_main, BF16, tn=512)
    graw = norm_matmul(xt, mix_norm[0], w_gate, F32, tn=w)
    igs, bcum = gate_prep(graw, bias)
    hmix = mlstm_core(proj, igs, bcum, mlstm_g_h[0], batch, seq)
    xt = matmul_residual(hmix, mlstm_w_out[0].astype(BF16), xt)

    xt = ffn_dense(xt, ffn_norm[0], ffn_w_gate[0].astype(BF16), ffn_w_up[0].astype(BF16), ffn_w_down[0].astype(BF16))

    sb_width = SB_HEADS * SB_HEAD_DIM
    kt = key_proj(xt, kv_norm, w_kv[:, :sb_width].astype(BF16), g_k, batch, seq)
    v = norm_matmul(xt, kv_norm, w_kv[:, sb_width:].astype(BF16), BF16, tn=512)
    qraw = norm_matmul(xt, mix_norm[1], sb_w_q[0].astype(BF16), F32, tn=512)
    o = stick_breaking(qraw, sb_g_q[0], kt, v, batch, seq)
    xt = matmul_residual(o, sb_w_o[0].astype(BF16), xt)

    xt = moe_routed(xt, ffn_norm[1], moe_w_router[0], moe_w_gate[0].astype(BF16), moe_w_up[0].astype(BF16), moe_w_down[0].astype(BF16))
    return xt.reshape(batch, seq, d)
```

```python
import functools

import jax
import jax.numpy as jnp
from jax import lax
from jax.experimental import pallas as pl
from jax.experimental.pallas import tpu as pltpu

F32 = jnp.float32
BF16 = jnp.bfloat16

EPS = 1e-6
GATE_SOFTCAP = 15.0
MLSTM_HEADS = 8
MLSTM_QK_DIM = 64
MLSTM_V_DIM = 128
SB_HEADS = 16
SB_HEAD_DIM = 64
N_EXPERTS = 8

V7X_LANES = 128
V7X_VMEM_BYTES = 64 * 1024 * 1024

ROW_TILE = 512
PROJ_ROW_TILE = 1024
PROJ_COL_TILE = 1024
FFN_ROW_TILE = 1024
FFN_COL_TILE = 256
MOE_COL_TILE = 512
MLSTM_TILE = 256
MOE_GROUP_TILE = 1024
MOE_ROUTE_TILE = 256
SB_Q_TILE = 512
SB_K_TILE = 256


def _vmem_limit(nbytes):
    return int(min(max(2 * nbytes, 32 * 1024 * 1024), V7X_VMEM_BYTES - 8 * 1024 * 1024))


def _params(semantics, nbytes):
    return pltpu.CompilerParams(dimension_semantics=semantics, vmem_limit_bytes=_vmem_limit(nbytes))


def _rms_scale(x, g):
    ms = jnp.mean(x * x, axis=-1, keepdims=True)
    return x * lax.rsqrt(ms + EPS) * g


def _norm_matmul_kernel(x_ref, g_ref, w_ref, o_ref, xn_ref):
    @pl.when(pl.program_id(1) == 0)
    def _():
        xn_ref[...] = _rms_scale(x_ref[...], g_ref[...]).astype(BF16)

    o_ref[...] = jnp.dot(xn_ref[...], w_ref[...], preferred_element_type=F32).astype(o_ref.dtype)


def norm_matmul(x, g, w, out_dtype, tn):
    t, d = x.shape
    n = w.shape[1]
    tm = PROJ_ROW_TILE
    nbytes = 2 * (tm * d * 4 + d * tn * 2 + tm * tn * 4) + tm * d * 2
    return pl.pallas_call(
        _norm_matmul_kernel,
        grid=(t // tm, n // tn),
        in_specs=[
            pl.BlockSpec((tm, d), lambda i, j: (i, 0)),
            pl.BlockSpec((1, d), lambda i, j: (0, 0)),
            pl.BlockSpec((d, tn), lambda i, j: (0, j)),
        ],
        out_specs=pl.BlockSpec((tm, tn), lambda i, j: (i, j)),
        out_shape=jax.ShapeDtypeStruct((t, n), out_dtype),
        scratch_shapes=[pltpu.VMEM((tm, d), BF16)],
        compiler_params=_params(("parallel", "arbitrary"), nbytes),
        name="norm_matmul",
    )(x, g.reshape(1, d), w)


def _matmul_residual_kernel(a_ref, w_ref, r_ref, o_ref):
    o_ref[...] = r_ref[...] + jnp.dot(a_ref[...], w_ref[...], preferred_element_type=F32)


def matmul_residual(a, w, res):
    t, k = a.shape
    n = w.shape[1]
    tm = PROJ_ROW_TILE
    nbytes = 2 * (tm * k * 2 + k * n * 2 + 2 * tm * n * 4)
    return pl.pallas_call(
        _matmul_residual_kernel,
        grid=(t // tm,),
        in_specs=[
            pl.BlockSpec((tm, k), lambda i: (i, 0)),
            pl.BlockSpec((k, n), lambda i: (0, 0)),
            pl.BlockSpec((tm, n), lambda i: (i, 0)),
        ],
        out_specs=pl.BlockSpec((tm, n), lambda i: (i, 0)),
        out_shape=jax.ShapeDtypeStruct((t, n), F32),
        compiler_params=_params(("parallel",), nbytes),
        name="matmul_residual",
    )(a, w, res)


def _split3(x):
    hi = x.astype(BF16)
    r1 = x - hi.astype(F32)
    mid = r1.astype(BF16)
    lo = (r1 - mid.astype(F32)).astype(BF16)
    return hi, mid, lo


def _gate_prep_kernel(g_ref, b_ref, tri_ref, ig_ref, bc_ref):
    z = g_ref[...] + b_ref[...]
    z = GATE_SOFTCAP * jnp.tanh(z / GATE_SOFTCAP)
    ig_ref[...] = z
    lf = jax.nn.log_sigmoid(z)
    hi, mid, lo = _split3(lf)
    parts = jnp.dot(tri_ref[...], jnp.concatenate([hi, mid, lo], axis=1), preferred_element_type=F32)
    w = V7X_LANES
    bc_ref[...] = (parts[:, :w] + parts[:, w:2 * w]) + parts[:, 2 * w:]


def gate_prep(graw, bias):
    t, w = graw.shape
    tl = MLSTM_TILE
    r = lax.broadcasted_iota(jnp.int32, (tl, tl), 0)
    c = lax.broadcasted_iota(jnp.int32, (tl, tl), 1)
    tri = (c <= r).astype(BF16)
    spec = pl.BlockSpec((tl, w), lambda i: (i, 0))
    return pl.pallas_call(
        _gate_prep_kernel,
        grid=(t // tl,),
        in_specs=[spec, pl.BlockSpec((1, w), lambda i: (0, 0)), pl.BlockSpec((tl, tl), lambda i: (0, 0))],
        out_specs=[spec, spec],
        out_shape=[jax.ShapeDtypeStruct((t, w), F32)] * 2,
        compiler_params=_params(("parallel",), 8 * tl * w * 4),
        name="gate_prep",
    )(graw, bias, tri)


def _mlstm_head(head, q_pair, k_pair, v, og, ig, bc, gh, causal, state_ref):
    tl, dv = v.shape
    lane = lax.broadcasted_iota(jnp.int32, (tl, V7X_LANES), 1)
    mine = (lane // MLSTM_QK_DIM) == (head % 2)
    zero = jnp.zeros((), BF16)
    q = jnp.where(mine, q_pair, zero) * jnp.asarray(MLSTM_QK_DIM ** -0.5, BF16)
    k = jnp.where(mine, k_pair, zero)
    assert dv == V7X_LANES
    b_last = bc[tl - 1:tl, :]
    bc = jnp.broadcast_to(bc, (tl, dv))
    ig = jnp.broadcast_to(ig, (tl, dv))

    src = (bc - ig).T[0:1, :]
    decay = jnp.exp(jnp.where(causal, jnp.concatenate([bc] * (tl // dv), axis=1) - src, -jnp.inf))

    s = lax.dot_general(q, k, (((1,), (1,)), ((), ())), preferred_element_type=F32)
    p = (s * decay).astype(BF16)
    v_aug = jnp.concatenate([v, jnp.ones((tl, dv), BF16)], axis=1)
    st = state_ref[...]
    r = jnp.dot(p, v_aug, preferred_element_type=F32)
    eb = jnp.exp(bc)
    r = r + jnp.concatenate([eb, eb], axis=1) * jnp.dot(q, st.astype(BF16), preferred_element_type=F32)
    num = r[:, :dv]
    den = r[:, dv:]
    h = num / jnp.maximum(jnp.abs(den), 1.0)
    h = h * lax.rsqrt(jnp.mean(h * h, axis=-1, keepdims=True) + EPS)
    h = h * gh * jax.nn.sigmoid(og.astype(F32))

    w = jnp.exp(b_last - bc + ig)
    vw = jnp.concatenate([v.astype(F32) * w, w], axis=1).astype(BF16)
    upd = lax.dot_general(k, vw, (((0,), (0,)), ((), ())), preferred_element_type=F32)
    state_ref[...] = jnp.exp(b_last) * st + upd
    return h


def _mlstm_kernel(q_ref, k_ref, v_ref, og_ref, ig_ref, bc_ref, gh_ref, o_ref, state_ref):
    tl = q_ref.shape[0]
    dv = MLSTM_V_DIM
    w = V7X_LANES

    @pl.when(pl.program_id(1) == 0)
    def _():
        state_ref[...] = jnp.zeros_like(state_ref)

    row = lax.broadcasted_iota(jnp.int32, (tl, tl), 0)
    col = lax.broadcasted_iota(jnp.int32, (tl, tl), 1)
    causal = col <= row
    for head in range(MLSTM_HEADS):
        pair = slice((head // 2) * w, (head // 2 + 1) * w)
        mine = slice(head * dv, (head + 1) * dv)
        h = _mlstm_head(head, q_ref[:, pair], k_ref[:, pair], v_ref[:, mine], og_ref[:, mine],
                        ig_ref[:, head:head + 1], bc_ref[:, MLSTM_HEADS + head:MLSTM_HEADS + head + 1],
                        gh_ref[:, mine], causal, state_ref.at[head])
        o_ref[:, mine] = h.astype(o_ref.dtype)


def mlstm_core(proj, igs, bcum, g_h, batch, seq):
    t = proj.shape[0]
    tl = MLSTM_TILE
    nt = seq // tl
    heads = MLSTM_HEADS
    dv = MLSTM_V_DIM
    w = V7X_LANES
    qk = heads * MLSTM_QK_DIM
    hv = heads * dv

    def rows(b, i):
        return b * nt + i

    nbytes = 2 * (2 * tl * qk * 2 + 3 * tl * hv * 2 + 2 * tl * w * 4) + heads * w * 2 * dv * 4 + 16 * tl * tl * 4
    return pl.pallas_call(
        _mlstm_kernel,
        grid=(batch, nt),
        in_specs=[
            pl.BlockSpec((tl, qk), lambda b, i: (rows(b, i), 0)),
            pl.BlockSpec((tl, qk), lambda b, i: (rows(b, i), 1)),
            pl.BlockSpec((tl, hv), lambda b, i: (rows(b, i), 2 * qk // hv)),
            pl.BlockSpec((tl, hv), lambda b, i: (rows(b, i), 2 * qk // hv + 1)),
            pl.BlockSpec((tl, w), lambda b, i: (rows(b, i), 0)),
            pl.BlockSpec((tl, w), lambda b, i: (rows(b, i), 0)),
            pl.BlockSpec((1, hv), lambda b, i: (0, 0)),
        ],
        out_specs=pl.BlockSpec((tl, hv), lambda b, i: (rows(b, i), 0)),
        out_shape=jax.ShapeDtypeStruct((t, hv), BF16),
        scratch_shapes=[pltpu.VMEM((heads, w, 2 * dv), F32)],
        compiler_params=_params(("parallel", "arbitrary"), nbytes),
        name="mlstm_core",
    )(proj, proj, proj, proj, igs, bcum, g_h.reshape(1, hv))


def _key_proj_kernel(x_ref, g_ref, w_ref, gk_ref, o_ref, xn_ref):
    @pl.when(pl.program_id(1) == 0)
    def _():
        xn_ref[...] = _rms_scale(x_ref[...], g_ref[...]).astype(BF16)

    y = jnp.dot(xn_ref[...], w_ref[...], preferred_element_type=F32)
    d = SB_HEAD_DIM
    for c in range(y.shape[1] // V7X_LANES):
        yt = y[:, c * V7X_LANES:(c + 1) * V7X_LANES].T
        for half in range(V7X_LANES // d):
            blk = yt[half * d:(half + 1) * d, :]
            ms = jnp.mean(blk * blk, axis=0, keepdims=True)
            kn = blk * lax.rsqrt(ms + EPS) * gk_ref[...]
            o_ref[0, c, half * d:(half + 1) * d, :] = kn.astype(o_ref.dtype)


def key_proj(x, g, w, g_k, batch, seq):
    t, d = x.shape
    n = w.shape[1]
    tm = ROW_TILE
    tn = 512
    ns = seq // tm
    pairs = n // V7X_LANES
    nbytes = 2 * (tm * d * 4 + d * tn * 2 + tm * tn * 2) + tm * d * 2 + 2 * tm * tn * 4
    return pl.pallas_call(
        _key_proj_kernel,
        grid=(t // tm, n // tn),
        in_specs=[
            pl.BlockSpec((tm, d), lambda i, j: (i, 0)),
            pl.BlockSpec((1, d), lambda i, j: (0, 0)),
            pl.BlockSpec((d, tn), lambda i, j: (0, j)),
            pl.BlockSpec((SB_HEAD_DIM, 1), lambda i, j: (0, 0)),
        ],
        out_specs=pl.BlockSpec((1, tn // V7X_LANES, V7X_LANES, tm), lambda i, j: (i // ns, j, 0, i % ns)),
        out_shape=jax.ShapeDtypeStruct((batch, pairs, V7X_LANES, seq), BF16),
        scratch_shapes=[pltpu.VMEM((tm, d), BF16)],
        compiler_params=_params(("parallel", "arbitrary"), nbytes),
        name="key_proj",
    )(x, g.reshape(1, d), w, g_k.reshape(SB_HEAD_DIM, 1))


LOG2E = 1.4426950408889634


def _sb_scores(qh, kt, z_ref):
    z_ref[...] = jnp.dot(qh, kt, preferred_element_type=F32)


def _sb_sticks(z_ref, upper, run, first_valid_col, zs_ref, bt_ref):
    tq, tk = z_ref.shape
    z = z_ref[...]
    sp = jnp.maximum(z, 0.0) + jnp.log(1.0 + jnp.exp2(-jnp.abs(z))) * LOG2E
    zs = (z - sp) - run
    if first_valid_col is not None:
        valid = lax.broadcasted_iota(jnp.int32, (tq, tk), 1) < first_valid_col
        sp = jnp.where(valid, sp, 0.0)
        zs = jnp.where(valid, zs, -jnp.inf)
    zs_ref[...] = zs
    bt_ref[...] = jnp.dot(sp.astype(BF16), upper, preferred_element_type=F32)
    return run + jnp.sum(sp, axis=-1, keepdims=True)


def _sb_values(zs_ref, bt_ref, vb, acc_ref):
    a = jnp.exp2(zs_ref[...] - bt_ref[...])
    acc_ref[...] += jnp.dot(a.astype(BF16), vb, preferred_element_type=F32)


def _sb_kernel(q_ref, gq_ref, kt_ref, v_ref, up_ref, o_ref, z_ref, zs_ref, bt_ref, acc_ref):
    i = pl.program_id(2)
    tq = q_ref.shape[0]
    tk = up_ref.shape[0]
    assert tq == 2 * tk, "the schedule below visits the key blocks of a query tile in pairs"
    d = SB_HEAD_DIM
    x = q_ref[...]
    lane = lax.broadcasted_iota(jnp.int32, x.shape, 1)
    first = lane < d
    x2 = x * x
    ss_a = jnp.sum(jnp.where(first, x2, 0.0), axis=-1, keepdims=True)
    ss_b = jnp.sum(jnp.where(first, 0.0, x2), axis=-1, keepdims=True)
    inv = jnp.where(first, lax.rsqrt(ss_a / d + EPS), lax.rsqrt(ss_b / d + EPS))
    qn = x * inv * gq_ref[...] * (LOG2E * d ** -0.5)
    heads = (jnp.where(first, qn, 0.0).astype(BF16), jnp.where(first, 0.0, qn).astype(BF16))
    upper = up_ref[...]
    row = lax.broadcasted_iota(jnp.int32, (tq, 1), 0)
    n_blocks = 2 * i + 2

    def start(n):
        kb = jnp.maximum(n_blocks - 1 - n, 0)
        return pl.multiple_of(kb * tk, tk)

    def scores(n, slot):
        kt = kt_ref[0, 0, :, pl.ds(start(n), tk)]
        for h in range(2):
            _sb_scores(heads[h], kt, z_ref.at[slot, h])

    def sticks(slot, runs, first_valid_col):
        return tuple(_sb_sticks(z_ref.at[slot, h], upper, runs[h], first_valid_col, zs_ref.at[slot, h],
                                bt_ref.at[slot, h]) for h in range(2))

    def values(n, slot):
        vb = v_ref[pl.ds(start(n), tk), :]
        for h in range(2):
            _sb_values(zs_ref.at[slot, h], bt_ref.at[slot, h], vb, acc_ref.at[h])

    acc_ref[...] = jnp.zeros_like(acc_ref)
    zero = jnp.zeros((tq, 1), F32)
    scores(0, 0)
    scores(1, 1)
    runs = sticks(0, (zero, zero), row - tk)
    scores(2, 0)
    runs = sticks(1, runs, row)
    values(0, 0)

    def body(m, runs):
        n = 2 + 2 * m
        scores(n + 1, 1)
        runs = sticks(0, runs, None)
        values(n - 1, 1)
        scores(n + 2, 0)
        runs = sticks(1, runs, None)
        values(n, 0)
        return runs

    lax.fori_loop(0, i, body, runs)
    values(n_blocks - 1, 1)
    o_ref[...] = jnp.where(first, acc_ref[0], acc_ref[1]).astype(o_ref.dtype)


def stick_breaking(qraw, g_q, kt, v, batch, seq):
    t, width = qraw.shape
    tq = SB_Q_TILE
    tk = SB_K_TILE
    nq = seq // tq
    w = V7X_LANES
    pairs = width // w
    r = lax.broadcasted_iota(jnp.int32, (tk, tk), 0)
    c = lax.broadcasted_iota(jnp.int32, (tk, tk), 1)
    upper = (r > c).astype(BF16)
    gq2 = jnp.tile(g_q.reshape(1, SB_HEAD_DIM), (1, w // SB_HEAD_DIM))
    nbytes = 2 * (tq * w * 4 + 2 * seq * w * 2 + tk * tk * 2 + tq * w * 2) + 32 * tq * tk * 4
    return pl.pallas_call(
        _sb_kernel,
        grid=(batch, pairs, nq),
        in_specs=[
            pl.BlockSpec((tq, w), lambda b, p, i: (b * nq + i, p)),
            pl.BlockSpec((1, w), lambda b, p, i: (0, 0)),
            pl.BlockSpec((1, 1, w, seq), lambda b, p, i: (b, p, 0, 0)),
            pl.BlockSpec((seq, w), lambda b, p, i: (b, p)),
            pl.BlockSpec((tk, tk), lambda b, p, i: (0, 0)),
        ],
        out_specs=pl.BlockSpec((tq, w), lambda b, p, i: (b * nq + i, p)),
        out_shape=jax.ShapeDtypeStruct((t, width), BF16),
        scratch_shapes=[
            pltpu.VMEM((2, 2, tq, tk), F32),
            pltpu.VMEM((2, 2, tq, tk), F32),
            pltpu.VMEM((2, 2, tq, tk), F32),
            pltpu.VMEM((2, tq, w), F32),
        ],
        compiler_params=_params(("parallel", "parallel", "arbitrary"), nbytes),
        name="stick_breaking",
    )(qraw, gq2, kt, v, upper)


def _ffn_kernel(x_ref, g_ref, wg_ref, wu_ref, wd_ref, o_ref, xn_ref, acc_ref):
    f = pl.program_id(1)

    @pl.when(f == 0)
    def _():
        xn_ref[...] = _rms_scale(x_ref[...], g_ref[...]).astype(BF16)
        acc_ref[...] = jnp.zeros_like(acc_ref)

    xn = xn_ref[...]
    hg = jnp.dot(xn, wg_ref[...], preferred_element_type=F32)
    hu = jnp.dot(xn, wu_ref[...], preferred_element_type=F32)
    act = (jax.nn.silu(hg) * hu).astype(BF16)
    acc_ref[...] += jnp.dot(act, wd_ref[...], preferred_element_type=F32)

    @pl.when(f == pl.num_programs(1) - 1)
    def _():
        o_ref[...] = x_ref[...] + acc_ref[...]


def ffn_dense(x, g, wg, wu, wd):
    t, d = x.shape
    f = wg.shape[1]
    tm = FFN_ROW_TILE
    tf = FFN_COL_TILE
    nbytes = 2 * (2 * tm * d * 4 + 3 * d * tf * 2) + tm * d * 6 + 4 * tm * tf * 4
    return pl.pallas_call(
        _ffn_kernel,
        grid=(t // tm, f // tf),
        in_specs=[
            pl.BlockSpec((tm, d), lambda i, j: (i, 0)),
            pl.BlockSpec((1, d), lambda i, j: (0, 0)),
            pl.BlockSpec((d, tf), lambda i, j: (0, j)),
            pl.BlockSpec((d, tf), lambda i, j: (0, j)),
            pl.BlockSpec((tf, d), lambda i, j: (j, 0)),
        ],
        out_specs=pl.BlockSpec((tm, d), lambda i, j: (i, 0)),
        out_shape=jax.ShapeDtypeStruct((t, d), F32),
        scratch_shapes=[pltpu.VMEM((tm, d), BF16), pltpu.VMEM((tm, d), F32)],
        compiler_params=_params(("parallel", "arbitrary"), nbytes),
        name="ffn_dense",
    )(x, g.reshape(1, d), wg, wu, wd)


def _router_kernel(x_ref, g_ref, wr_ref, tri_ref, xn_ref, mi_ref, mf_ref, cnt_ref, run_ref):
    @pl.when(pl.program_id(0) == 0)
    def _():
        run_ref[...] = jnp.zeros_like(run_ref)

    xn = _rms_scale(x_ref[...], g_ref[...])
    xn_ref[...] = xn
    logits = jnp.dot(xn.astype(BF16), wr_ref[...], preferred_element_type=F32)
    lane = lax.broadcasted_iota(jnp.int32, logits.shape, 1)
    neg = jnp.asarray(-jnp.inf, F32)
    logits = jnp.where(lane < N_EXPERTS, logits, neg)
    m1 = jnp.max(logits, axis=-1, keepdims=True)
    i1 = jnp.min(jnp.where(logits == m1, lane, V7X_LANES), axis=-1, keepdims=True)
    rest = jnp.where(lane == i1, neg, logits)
    m2 = jnp.max(rest, axis=-1, keepdims=True)
    i2 = jnp.min(jnp.where(rest == m2, lane, V7X_LANES), axis=-1, keepdims=True)
    e2 = jnp.exp(m2 - m1)
    denom = 1.0 + e2
    mf_ref[...] = jnp.where(lane == 0, 1.0 / denom, jnp.where(lane == 1, e2 / denom, 0.0))

    chosen = jnp.logical_or(lane == i1, lane == i2)
    onehot = jnp.where(chosen, 1.0, 0.0)
    rank = jnp.dot(tri_ref[...], onehot.astype(BF16), preferred_element_type=F32) + run_ref[...]
    r1 = jnp.sum(jnp.where(lane == i1, rank, 0.0), axis=-1, keepdims=True).astype(jnp.int32)
    r2 = jnp.sum(jnp.where(lane == i2, rank, 0.0), axis=-1, keepdims=True).astype(jnp.int32)
    mi_ref[...] = jnp.where(lane == 0, i1, jnp.where(lane == 1, i2, jnp.where(lane == 2, r1, jnp.where(lane == 3, r2, 0))))
    run_ref[...] += jnp.sum(onehot, axis=0, keepdims=True)
    cnt_ref[...] = run_ref[...]


def router(x, g, w_router_padded):
    t, d = x.shape
    tm = ROW_TILE
    w = V7X_LANES
    r = lax.broadcasted_iota(jnp.int32, (tm, tm), 0)
    c = lax.broadcasted_iota(jnp.int32, (tm, tm), 1)
    tri = (c < r).astype(BF16)
    nbytes = 2 * (2 * tm * d * 4 + d * w * 2 + tm * tm * 2 + 2 * tm * w * 4) + 2 * tm * d * 4
    row_spec = pl.BlockSpec((tm, w), lambda i: (i, 0))
    return pl.pallas_call(
        _router_kernel,
        grid=(t // tm,),
        in_specs=[
            pl.BlockSpec((tm, d), lambda i: (i, 0)),
            pl.BlockSpec((1, d), lambda i: (0, 0)),
            pl.BlockSpec((d, w), lambda i: (0, 0)),
            pl.BlockSpec((tm, tm), lambda i: (0, 0)),
        ],
        out_specs=[pl.BlockSpec((tm, d), lambda i: (i, 0)), row_spec, row_spec, pl.BlockSpec((1, w), lambda i: (0, 0))],
        out_shape=[
            jax.ShapeDtypeStruct((t, d), F32),
            jax.ShapeDtypeStruct((t, w), jnp.int32),
            jax.ShapeDtypeStruct((t, w), F32),
            jax.ShapeDtypeStruct((1, w), F32),
        ],
        scratch_shapes=[pltpu.VMEM((1, w), F32)],
        compiler_params=_params(("arbitrary",), nbytes),
        name="router",
    )(x, g.reshape(1, d), w_router_padded, tri)


def _row_copy(src, src_row, dst, dst_row, sem):
    return pltpu.make_async_copy(src.at[pl.ds(src_row, 1)], dst.at[pl.ds(dst_row, 1)], sem)


def _dispatch_kernel(slots_ref, xn_ref, zeros_hbm, xs_hbm, sem):
    del zeros_hbm
    td = xn_ref.shape[0]

    def body(r, carry):
        for k in range(2):
            _row_copy(xn_ref, r, xs_hbm, slots_ref[2 * r + k], sem).start()
        return carry

    lax.fori_loop(0, td, body, 0, unroll=8)
    for k in range(2):
        pltpu.make_async_copy(xn_ref, xs_hbm.at[pl.ds(0, td)], sem).wait()


def dispatch(slots, xn, n_slots):
    t, d = xn.shape
    td = MOE_ROUTE_TILE
    zeros = jnp.zeros((n_slots, d), xn.dtype)
    return pl.pallas_call(
        _dispatch_kernel,
        grid=(t // td,),
        in_specs=[
            pl.BlockSpec((2 * td,), lambda i: (i,), memory_space=pltpu.SMEM),
            pl.BlockSpec((td, d), lambda i: (i, 0)),
            pl.BlockSpec(memory_space=pl.ANY),
        ],
        out_specs=pl.BlockSpec(memory_space=pl.ANY),
        out_shape=jax.ShapeDtypeStruct((n_slots, d), xn.dtype),
        scratch_shapes=[pltpu.SemaphoreType.DMA],
        input_output_aliases={2: 0},
        compiler_params=_params(("arbitrary",), 2 * td * d * 4),
        name="moe_dispatch",
    )(slots, xn, zeros)


def _experts_kernel(te_ref, na_ref, xs_ref, wg_ref, wu_ref, wd_ref, ys_ref, xb_ref, acc_ref):
    del te_ref
    f = pl.program_id(1)
    active = pl.program_id(0) < na_ref[0]

    @pl.when(jnp.logical_and(jnp.logical_not(active), f == 0))
    def _():
        ys_ref[...] = jnp.zeros_like(ys_ref)

    @pl.when(active)
    def _():
        @pl.when(f == 0)
        def _():
            xb_ref[...] = xs_ref[...].astype(BF16)
            acc_ref[...] = jnp.zeros_like(acc_ref)

        xb = xb_ref[...]
        hg = jnp.dot(xb, wg_ref[0], preferred_element_type=F32)
        hu = jnp.dot(xb, wu_ref[0], preferred_element_type=F32)
        act = (jax.nn.silu(hg) * hu).astype(BF16)
        acc_ref[...] += jnp.dot(act, wd_ref[0], preferred_element_type=F32)

        @pl.when(f == pl.num_programs(1) - 1)
        def _():
            ys_ref[...] = acc_ref[...]


def experts(tile_expert, n_active, xs, wg, wu, wd):
    n_slots, d = xs.shape
    fe = wg.shape[2]
    tg = MOE_GROUP_TILE
    tf = MOE_COL_TILE
    nf = fe // tf
    n_tiles = n_slots // tg

    def tile(n, na):
        return jnp.minimum(n, na[0] - 1)

    def col(n, f, na):
        return jnp.where(n < na[0], f, nf - 1)

    nbytes = 2 * (2 * tg * d * 4 + 3 * d * tf * 2) + tg * d * 6 + 4 * tg * tf * 4
    return pl.pallas_call(
        _experts_kernel,
        grid_spec=pltpu.PrefetchScalarGridSpec(
            num_scalar_prefetch=2,
            grid=(n_tiles, nf),
            in_specs=[
                pl.BlockSpec((tg, d), lambda n, f, te, na: (tile(n, na), 0)),
                pl.BlockSpec((1, d, tf), lambda n, f, te, na: (te[tile(n, na)], 0, col(n, f, na))),
                pl.BlockSpec((1, d, tf), lambda n, f, te, na: (te[tile(n, na)], 0, col(n, f, na))),
                pl.BlockSpec((1, tf, d), lambda n, f, te, na: (te[tile(n, na)], col(n, f, na), 0)),
            ],
            out_specs=pl.BlockSpec((tg, d), lambda n, f, te, na: (n, 0)),
            scratch_shapes=[pltpu.VMEM((tg, d), BF16), pltpu.VMEM((tg, d), F32)],
        ),
        out_shape=jax.ShapeDtypeStruct((n_slots, d), F32),
        compiler_params=_params(("arbitrary", "arbitrary"), nbytes),
        name="moe_experts",
    )(tile_expert, n_active, xs, wg, wu, wd)


def _combine_kernel(slots_ref, x_ref, mf_ref, ys_hbm, o_ref, buf_ref, sem):
    tc = x_ref.shape[0]

    def body(r, carry):
        for k in range(2):
            _row_copy(ys_hbm, slots_ref[2 * r + k], buf_ref.at[k], r, sem).start()
        return carry

    lax.fori_loop(0, tc, body, 0, unroll=8)
    for k in range(2):
        pltpu.make_async_copy(ys_hbm.at[pl.ds(0, tc)], buf_ref.at[k], sem).wait()
    lane = lax.broadcasted_iota(jnp.int32, mf_ref.shape, 1)
    mf = mf_ref[...]
    w1 = jnp.sum(jnp.where(lane == 0, mf, 0.0), axis=-1, keepdims=True)
    w2 = jnp.sum(jnp.where(lane == 1, mf, 0.0), axis=-1, keepdims=True)
    o_ref[...] = x_ref[...] + (w1 * buf_ref[0] + w2 * buf_ref[1])


def combine(slots, x, meta_f, ys):
    t, d = x.shape
    tc = MOE_ROUTE_TILE
    w = meta_f.shape[1]
    nbytes = 2 * (2 * tc * d * 4 + tc * w * 4) + 2 * tc * d * 4
    return pl.pallas_call(
        _combine_kernel,
        grid=(t // tc,),
        in_specs=[
            pl.BlockSpec((2 * tc,), lambda i: (i,), memory_space=pltpu.SMEM),
            pl.BlockSpec((tc, d), lambda i: (i, 0)),
            pl.BlockSpec((tc, w), lambda i: (i, 0)),
            pl.BlockSpec(memory_space=pl.ANY),
        ],
        out_specs=pl.BlockSpec((tc, d), lambda i: (i, 0)),
        out_shape=jax.ShapeDtypeStruct((t, d), F32),
        scratch_shapes=[pltpu.VMEM((2, tc, d), F32), pltpu.SemaphoreType.DMA],
        compiler_params=_params(("arbitrary",), nbytes),
        name="moe_combine",
    )(slots, x, meta_f, ys)


def moe_routed(x, g, w_router, wg, wu, wd):
    t, d = x.shape
    w = V7X_LANES
    tg = MOE_GROUP_TILE
    n_tiles = 2 * t // tg + N_EXPERTS
    w_r = jnp.pad(w_router, ((0, 0), (0, w - N_EXPERTS))).astype(BF16)
    xn, meta_i, meta_f, counts = router(x, g, w_r)

    counts = counts[0, :N_EXPERTS].astype(jnp.int32)
    tiles_per = (counts + tg - 1) // tg
    tile_end = jnp.cumsum(tiles_per)
    group_start = (tile_end - tiles_per) * tg
    n_active = tile_end[-1:].astype(jnp.int32)
    tile_ids = jnp.arange(n_tiles, dtype=jnp.int32)
    tile_expert = jnp.minimum(jnp.sum(tile_ids[:, None] >= tile_end[None, :], axis=1), N_EXPERTS - 1).astype(jnp.int32)
    slots = (jnp.take(group_start, meta_i[:, 0:2]) + meta_i[:, 2:4]).reshape(-1).astype(jnp.int32)

    xs = dispatch(slots, xn, n_tiles * tg)
    ys = experts(tile_expert, n_active, xs, wg, wu, wd)
    return combine(slots, x, meta_f, ys)


def kernel(x, mix_norm, ffn_norm, mlstm_w_in, mlstm_b_igate, mlstm_b_fgate, mlstm_g_h, mlstm_w_out, kv_norm, w_kv, g_k, sb_w_q, sb_g_q, sb_w_o, ffn_w_gate, ffn_w_up, ffn_w_down, moe_w_router, moe_w_gate, moe_w_up, moe_w_down):
    batch, seq, d = x.shape
    xt = x.reshape(batch * seq, d)
    w = V7X_LANES
    n_main = 2 * MLSTM_HEADS * MLSTM_QK_DIM + 2 * MLSTM_HEADS * MLSTM_V_DIM

    w_in = mlstm_w_in[0]
    w_main = w_in[:, :n_main].astype(BF16)
    w_gate = jnp.pad(w_in[:, n_main:], ((0, 0), (0, w - 2 * MLSTM_HEADS))).astype(BF16)
    bias = jnp.pad(jnp.concatenate([mlstm_b_igate[0], mlstm_b_fgate[0]]), (0, w - 2 * MLSTM_HEADS)).reshape(1, w)
    proj = norm_matmul(xt, mix_norm[0], w_main, BF16, tn=PROJ_COL_TILE)
    graw = norm_matmul(xt, mix_norm[0], w_gate, F32, tn=w)
    igs, bcum = gate_prep(graw, bias)
    hmix = mlstm_core(proj, igs, bcum, mlstm_g_h[0], batch, seq)
    xt = matmul_residual(hmix, mlstm_w_out[0].astype(BF16), xt)

    xt = ffn_dense(xt, ffn_norm[0], ffn_w_gate[0].astype(BF16), ffn_w_up[0].astype(BF16), ffn_w_down[0].astype(BF16))

    sb_width = SB_HEADS * SB_HEAD_DIM
    kt = key_proj(xt, kv_norm, w_kv[:, :sb_width].astype(BF16), g_k, batch, seq)
    v = norm_matmul(xt, kv_norm, w_kv[:, sb_width:].astype(BF16), BF16, tn=PROJ_COL_TILE)
    qraw = norm_matmul(xt, mix_norm[1], sb_w_q[0].astype(BF16), F32, tn=PROJ_COL_TILE)
    o = stick_breaking(qraw, sb_g_q[0], kt, v, batch, seq)
    xt = matmul_residual(o, sb_w_o[0].astype(BF16), xt)

    xt = moe_routed(xt, ffn_norm[1], moe_w_router[0], moe_w_gate[0].astype(BF16), moe_w_up[0].astype(BF16), moe_w_down[0].astype(BF16))
    return xt.reshape(batch, seq, d)
```

```python
import functools

import jax
import jax.numpy as jnp
from jax import lax
from jax.experimental import pallas as pl
from jax.experimental.pallas import tpu as pltpu

F32 = jnp.float32
BF16 = jnp.bfloat16

EPS = 1e-6
GATE_SOFTCAP = 15.0
MLSTM_HEADS = 8
MLSTM_QK_DIM = 64
MLSTM_V_DIM = 128
SB_HEADS = 16
SB_HEAD_DIM = 64
N_EXPERTS = 8

V7X_LANES = 128
V7X_VMEM_BYTES = 64 * 1024 * 1024

ROW_TILE = 512
PROJ_ROW_TILE = 1024
PROJ_COL_TILE = 1024
FFN_ROW_TILE = 1024
FFN_COL_TILE = 256
MOE_COL_TILE = 512
MLSTM_TILE = 256
MOE_GROUP_TILE = 1024
MOE_ROUTE_TILE = 256
SB_Q_TILE = 512
SB_K_TILE = 256


def _vmem_limit(nbytes):
    return int(min(max(2 * nbytes, 32 * 1024 * 1024), V7X_VMEM_BYTES - 8 * 1024 * 1024))


def _params(semantics, nbytes):
    return pltpu.CompilerParams(dimension_semantics=semantics, vmem_limit_bytes=_vmem_limit(nbytes))


def _rms_scale(x, g):
    ms = jnp.mean(x * x, axis=-1, keepdims=True)
    return x * lax.rsqrt(ms + EPS) * g


def _norm_matmul_kernel(x_ref, g_ref, w_ref, o_ref, xn_ref):
    @pl.when(pl.program_id(1) == 0)
    def _():
        xn_ref[...] = _rms_scale(x_ref[...], g_ref[...]).astype(BF16)

    o_ref[...] = jnp.dot(xn_ref[...], w_ref[...], preferred_element_type=F32).astype(o_ref.dtype)


def norm_matmul(x, g, w, out_dtype, tn):
    t, d = x.shape
    n = w.shape[1]
    tm = PROJ_ROW_TILE
    nbytes = 2 * (tm * d * 4 + d * tn * 2 + tm * tn * 4) + tm * d * 2
    return pl.pallas_call(
        _norm_matmul_kernel,
        grid=(t // tm, n // tn),
        in_specs=[
            pl.BlockSpec((tm, d), lambda i, j: (i, 0)),
            pl.BlockSpec((1, d), lambda i, j: (0, 0)),
            pl.BlockSpec((d, tn), lambda i, j: (0, j)),
        ],
        out_specs=pl.BlockSpec((tm, tn), lambda i, j: (i, j)),
        out_shape=jax.ShapeDtypeStruct((t, n), out_dtype),
        scratch_shapes=[pltpu.VMEM((tm, d), BF16)],
        compiler_params=_params(("parallel", "arbitrary"), nbytes),
        name="norm_matmul",
    )(x, g.reshape(1, d), w)


def _matmul_residual_kernel(a_ref, w_ref, r_ref, o_ref):
    o_ref[...] = r_ref[...] + jnp.dot(a_ref[...], w_ref[...], preferred_element_type=F32)


def matmul_residual(a, w, res):
    t, k = a.shape
    n = w.shape[1]
    tm = PROJ_ROW_TILE
    nbytes = 2 * (tm * k * 2 + k * n * 2 + 2 * tm * n * 4)
    return pl.pallas_call(
        _matmul_residual_kernel,
        grid=(t // tm,),
        in_specs=[
            pl.BlockSpec((tm, k), lambda i: (i, 0)),
            pl.BlockSpec((k, n), lambda i: (0, 0)),
            pl.BlockSpec((tm, n), lambda i: (i, 0)),
        ],
        out_specs=pl.BlockSpec((tm, n), lambda i: (i, 0)),
        out_shape=jax.ShapeDtypeStruct((t, n), F32),
        compiler_params=_params(("parallel",), nbytes),
        name="matmul_residual",
    )(a, w, res)


def _split3(x):
    hi = x.astype(BF16)
    r1 = x - hi.astype(F32)
    mid = r1.astype(BF16)
    lo = (r1 - mid.astype(F32)).astype(BF16)
    return hi, mid, lo


def _gate_prep_kernel(g_ref, b_ref, tri_ref, ig_ref, bc_ref):
    z = g_ref[...] + b_ref[...]
    z = GATE_SOFTCAP * jnp.tanh(z / GATE_SOFTCAP)
    ig_ref[...] = z
    lf = jax.nn.log_sigmoid(z)
    hi, mid, lo = _split3(lf)
    parts = jnp.dot(tri_ref[...], jnp.concatenate([hi, mid, lo], axis=1), preferred_element_type=F32)
    w = V7X_LANES
    bc_ref[...] = (parts[:, :w] + parts[:, w:2 * w]) + parts[:, 2 * w:]


def gate_prep(graw, bias):
    t, w = graw.shape
    tl = MLSTM_TILE
    r = lax.broadcasted_iota(jnp.int32, (tl, tl), 0)
    c = lax.broadcasted_iota(jnp.int32, (tl, tl), 1)
    tri = (c <= r).astype(BF16)
    spec = pl.BlockSpec((tl, w), lambda i: (i, 0))
    return pl.pallas_call(
        _gate_prep_kernel,
        grid=(t // tl,),
        in_specs=[spec, pl.BlockSpec((1, w), lambda i: (0, 0)), pl.BlockSpec((tl, tl), lambda i: (0, 0))],
        out_specs=[spec, spec],
        out_shape=[jax.ShapeDtypeStruct((t, w), F32)] * 2,
        compiler_params=_params(("parallel",), 8 * tl * w * 4),
        name="gate_prep",
    )(graw, bias, tri)


def _mlstm_head(head, q_pair, k_pair, v, og, ig, bc, gh, causal, state_ref):
    tl, dv = v.shape
    lane = lax.broadcasted_iota(jnp.int32, (tl, V7X_LANES), 1)
    mine = (lane // MLSTM_QK_DIM) == (head % 2)
    zero = jnp.zeros((), BF16)
    q = jnp.where(mine, q_pair, zero) * jnp.asarray(MLSTM_QK_DIM ** -0.5, BF16)
    k = jnp.where(mine, k_pair, zero)
    assert dv == V7X_LANES
    b_last = bc[tl - 1:tl, :]
    bc = jnp.broadcast_to(bc, (tl, dv))
    ig = jnp.broadcast_to(ig, (tl, dv))

    src = (bc - ig).T[0:1, :]
    decay = jnp.exp(jnp.where(causal, jnp.concatenate([bc] * (tl // dv), axis=1) - src, -jnp.inf))

    s = lax.dot_general(q, k, (((1,), (1,)), ((), ())), preferred_element_type=F32)
    p = (s * decay).astype(BF16)
    v_aug = jnp.concatenate([v, jnp.ones((tl, dv), BF16)], axis=1)
    st = state_ref[...]
    r = jnp.dot(p, v_aug, preferred_element_type=F32)
    eb = jnp.exp(bc)
    r = r + jnp.concatenate([eb, eb], axis=1) * jnp.dot(q, st.astype(BF16), preferred_element_type=F32)
    num = r[:, :dv]
    den = r[:, dv:]
    h = num / jnp.maximum(jnp.abs(den), 1.0)
    h = h * lax.rsqrt(jnp.mean(h * h, axis=-1, keepdims=True) + EPS)
    h = h * gh * jax.nn.sigmoid(og.astype(F32))

    w = jnp.exp(b_last - bc + ig)
    vw = jnp.concatenate([v.astype(F32) * w, w], axis=1).astype(BF16)
    upd = lax.dot_general(k, vw, (((0,), (0,)), ((), ())), preferred_element_type=F32)
    state_ref[...] = jnp.exp(b_last) * st + upd
    return h


def _mlstm_kernel(q_ref, k_ref, v_ref, og_ref, ig_ref, bc_ref, gh_ref, o_ref, state_ref):
    tl = q_ref.shape[0]
    dv = MLSTM_V_DIM
    w = V7X_LANES

    @pl.when(pl.program_id(1) == 0)
    def _():
        state_ref[...] = jnp.zeros_like(state_ref)

    row = lax.broadcasted_iota(jnp.int32, (tl, tl), 0)
    col = lax.broadcasted_iota(jnp.int32, (tl, tl), 1)
    causal = col <= row
    for head in range(MLSTM_HEADS):
        pair = slice((head // 2) * w, (head // 2 + 1) * w)
        mine = slice(head * dv, (head + 1) * dv)
        h = _mlstm_head(head, q_ref[:, pair], k_ref[:, pair], v_ref[:, mine], og_ref[:, mine],
                        ig_ref[:, head:head + 1], bc_ref[:, MLSTM_HEADS + head:MLSTM_HEADS + head + 1],
                        gh_ref[:, mine], causal, state_ref.at[head])
        o_ref[:, mine] = h.astype(o_ref.dtype)


def mlstm_core(proj, igs, bcum, g_h, batch, seq):
    t = proj.shape[0]
    tl = MLSTM_TILE
    nt = seq // tl
    heads = MLSTM_HEADS
    dv = MLSTM_V_DIM
    w = V7X_LANES
    qk = heads * MLSTM_QK_DIM
    hv = heads * dv

    def rows(b, i):
        return b * nt + i

    nbytes = 2 * (2 * tl * qk * 2 + 3 * tl * hv * 2 + 2 * tl * w * 4) + heads * w * 2 * dv * 4 + 16 * tl * tl * 4
    return pl.pallas_call(
        _mlstm_kernel,
        grid=(batch, nt),
        in_specs=[
            pl.BlockSpec((tl, qk), lambda b, i: (rows(b, i), 0)),
            pl.BlockSpec((tl, qk), lambda b, i: (rows(b, i), 1)),
            pl.BlockSpec((tl, hv), lambda b, i: (rows(b, i), 2 * qk // hv)),
            pl.BlockSpec((tl, hv), lambda b, i: (rows(b, i), 2 * qk // hv + 1)),
            pl.BlockSpec((tl, w), lambda b, i: (rows(b, i), 0)),
            pl.BlockSpec((tl, w), lambda b, i: (rows(b, i), 0)),
            pl.BlockSpec((1, hv), lambda b, i: (0, 0)),
        ],
        out_specs=pl.BlockSpec((tl, hv), lambda b, i: (rows(b, i), 0)),
        out_shape=jax.ShapeDtypeStruct((t, hv), BF16),
        scratch_shapes=[pltpu.VMEM((heads, w, 2 * dv), F32)],
        compiler_params=_params(("parallel", "arbitrary"), nbytes),
        name="mlstm_core",
    )(proj, proj, proj, proj, igs, bcum, g_h.reshape(1, hv))


def _key_proj_kernel(x_ref, g_ref, w_ref, gk_ref, o_ref, xn_ref):
    @pl.when(pl.program_id(1) == 0)
    def _():
        xn_ref[...] = _rms_scale(x_ref[...], g_ref[...]).astype(BF16)

    y = jnp.dot(xn_ref[...], w_ref[...], preferred_element_type=F32)
    d = SB_HEAD_DIM
    for c in range(y.shape[1] // V7X_LANES):
        yt = y[:, c * V7X_LANES:(c + 1) * V7X_LANES].T
        for half in range(V7X_LANES // d):
            blk = yt[half * d:(half + 1) * d, :]
            ms = jnp.mean(blk * blk, axis=0, keepdims=True)
            kn = blk * lax.rsqrt(ms + EPS) * gk_ref[...]
            o_ref[0, c, half * d:(half + 1) * d, :] = kn.astype(o_ref.dtype)


def key_proj(x, g, w, g_k, batch, seq):
    t, d = x.shape
    n = w.shape[1]
    tm = ROW_TILE
    tn = 512
    ns = seq // tm
    pairs = n // V7X_LANES
    nbytes = 2 * (tm * d * 4 + d * tn * 2 + tm * tn * 2) + tm * d * 2 + 2 * tm * tn * 4
    return pl.pallas_call(
        _key_proj_kernel,
        grid=(t // tm, n // tn),
        in_specs=[
            pl.BlockSpec((tm, d), lambda i, j: (i, 0)),
            pl.BlockSpec((1, d), lambda i, j: (0, 0)),
            pl.BlockSpec((d, tn), lambda i, j: (0, j)),
            pl.BlockSpec((SB_HEAD_DIM, 1), lambda i, j: (0, 0)),
        ],
        out_specs=pl.BlockSpec((1, tn // V7X_LANES, V7X_LANES, tm), lambda i, j: (i // ns, j, 0, i % ns)),
        out_shape=jax.ShapeDtypeStruct((batch, pairs, V7X_LANES, seq), BF16),
        scratch_shapes=[pltpu.VMEM((tm, d), BF16)],
        compiler_params=_params(("parallel", "arbitrary"), nbytes),
        name="key_proj",
    )(x, g.reshape(1, d), w, g_k.reshape(SB_HEAD_DIM, 1))


LOG2E = 1.4426950408889634


def _sb_scores(qh, kt, z_ref):
    z_ref[...] = jnp.dot(qh, kt, preferred_element_type=F32)


def _sb_sticks(z_ref, upper, run, first_valid_col, zs_ref, bt_ref):
    tq, tk = z_ref.shape
    z = z_ref[...]
    sp = jnp.maximum(z, 0.0) + jnp.log(1.0 + jnp.exp2(-jnp.abs(z))) * LOG2E
    zs = (z - sp) - run
    if first_valid_col is not None:
        valid = lax.broadcasted_iota(jnp.int32, (tq, tk), 1) < first_valid_col
        sp = jnp.where(valid, sp, 0.0)
        zs = jnp.where(valid, zs, -jnp.inf)
    zs_ref[...] = zs
    bt_ref[...] = jnp.dot(sp.astype(BF16), upper, preferred_element_type=F32)
    return run + jnp.sum(sp, axis=-1, keepdims=True)


def _sb_values(zs_ref, bt_ref, vb, acc_ref):
    a = jnp.exp2(zs_ref[...] - bt_ref[...])
    acc_ref[...] += jnp.dot(a.astype(BF16), vb, preferred_element_type=F32)


def _sb_kernel(q_ref, gq_ref, kt_ref, v_ref, up_ref, o_ref, z_ref, zs_ref, bt_ref, acc_ref):
    i = pl.program_id(2)
    tq = q_ref.shape[0]
    tk = up_ref.shape[0]
    assert tq == 2 * tk, "the schedule below visits the key blocks of a query tile in pairs"
    d = SB_HEAD_DIM
    x = q_ref[...]
    lane = lax.broadcasted_iota(jnp.int32, x.shape, 1)
    first = lane < d
    x2 = x * x
    ss_a = jnp.sum(jnp.where(first, x2, 0.0), axis=-1, keepdims=True)
    ss_b = jnp.sum(jnp.where(first, 0.0, x2), axis=-1, keepdims=True)
    inv = jnp.where(first, lax.rsqrt(ss_a / d + EPS), lax.rsqrt(ss_b / d + EPS))
    qn = x * inv * gq_ref[...] * (LOG2E * d ** -0.5)
    heads = (jnp.where(first, qn, 0.0).astype(BF16), jnp.where(first, 0.0, qn).astype(BF16))
    upper = up_ref[...]
    row = lax.broadcasted_iota(jnp.int32, (tq, 1), 0)
    n_blocks = 2 * i + 2

    def start(n):
        kb = jnp.maximum(n_blocks - 1 - n, 0)
        return pl.multiple_of(kb * tk, tk)

    everything = pl.ds(0, tq)
    second_half = pl.ds(tk, tk)

    def scores(n, slot, rows=everything):
        kt = kt_ref[0, 0, :, pl.ds(start(n), tk)]
        for h in range(2):
            _sb_scores(heads[h][rows.start:rows.start + rows.size], kt, z_ref.at[slot, h, rows])

    def sticks(slot, runs, first_valid_col, rows=everything):
        return tuple(_sb_sticks(z_ref.at[slot, h, rows], upper, runs[h], first_valid_col, zs_ref.at[slot, h, rows],
                                bt_ref.at[slot, h, rows]) for h in range(2))

    def values(n, slot, rows=everything):
        vb = v_ref[pl.ds(start(n), tk), :]
        for h in range(2):
            _sb_values(zs_ref.at[slot, h, rows], bt_ref.at[slot, h, rows], vb, acc_ref.at[h, rows])

    acc_ref[...] = jnp.zeros_like(acc_ref)
    half_zero = jnp.zeros((tk, 1), F32)
    scores(0, 0, second_half)
    scores(1, 1)
    runs = sticks(0, (half_zero, half_zero), row[:tk], second_half)
    runs = tuple(jnp.concatenate([half_zero, r], axis=0) for r in runs)
    scores(2, 0)
    runs = sticks(1, runs, row)
    values(0, 0, second_half)

    def body(m, runs):
        n = 2 + 2 * m
        scores(n + 1, 1)
        runs = sticks(0, runs, None)
        values(n - 1, 1)
        scores(n + 2, 0)
        runs = sticks(1, runs, None)
        values(n, 0)
        return runs

    lax.fori_loop(0, i, body, runs)
    values(n_blocks - 1, 1)
    o_ref[...] = jnp.where(first, acc_ref[0], acc_ref[1]).astype(o_ref.dtype)


def stick_breaking(qraw, g_q, kt, v, batch, seq):
    t, width = qraw.shape
    tq = SB_Q_TILE
    tk = SB_K_TILE
    nq = seq // tq
    w = V7X_LANES
    pairs = width // w
    r = lax.broadcasted_iota(jnp.int32, (tk, tk), 0)
    c = lax.broadcasted_iota(jnp.int32, (tk, tk), 1)
    upper = (r > c).astype(BF16)
    gq2 = jnp.tile(g_q.reshape(1, SB_HEAD_DIM), (1, w // SB_HEAD_DIM))
    nbytes = 2 * (tq * w * 4 + 2 * seq * w * 2 + tk * tk * 2 + tq * w * 2) + 32 * tq * tk * 4
    return pl.pallas_call(
        _sb_kernel,
        grid=(batch, pairs, nq),
        in_specs=[
            pl.BlockSpec((tq, w), lambda b, p, i: (b * nq + i, p)),
            pl.BlockSpec((1, w), lambda b, p, i: (0, 0)),
            pl.BlockSpec((1, 1, w, seq), lambda b, p, i: (b, p, 0, 0)),
            pl.BlockSpec((seq, w), lambda b, p, i: (b, p)),
            pl.BlockSpec((tk, tk), lambda b, p, i: (0, 0)),
        ],
        out_specs=pl.BlockSpec((tq, w), lambda b, p, i: (b * nq + i, p)),
        out_shape=jax.ShapeDtypeStruct((t, width), BF16),
        scratch_shapes=[
            pltpu.VMEM((2, 2, tq, tk), F32),
            pltpu.VMEM((2, 2, tq, tk), F32),
            pltpu.VMEM((2, 2, tq, tk), F32),
            pltpu.VMEM((2, tq, w), F32),
        ],
        compiler_params=_params(("parallel", "parallel", "arbitrary"), nbytes),
        name="stick_breaking",
    )(qraw, gq2, kt, v, upper)


def _ffn_kernel(x_ref, g_ref, wg_ref, wu_ref, wd_ref, o_ref, xn_ref, acc_ref):
    f = pl.program_id(1)

    @pl.when(f == 0)
    def _():
        xn_ref[...] = _rms_scale(x_ref[...], g_ref[...]).astype(BF16)
        acc_ref[...] = jnp.zeros_like(acc_ref)

    xn = xn_ref[...]
    hg = jnp.dot(xn, wg_ref[...], preferred_element_type=F32)
    hu = jnp.dot(xn, wu_ref[...], preferred_element_type=F32)
    act = (jax.nn.silu(hg) * hu).astype(BF16)
    acc_ref[...] += jnp.dot(act, wd_ref[...], preferred_element_type=F32)

    @pl.when(f == pl.num_programs(1) - 1)
    def _():
        o_ref[...] = x_ref[...] + acc_ref[...]


def ffn_dense(x, g, wg, wu, wd):
    t, d = x.shape
    f = wg.shape[1]
    tm = FFN_ROW_TILE
    tf = FFN_COL_TILE
    nbytes = 2 * (2 * tm * d * 4 + 3 * d * tf * 2) + tm * d * 6 + 4 * tm * tf * 4
    return pl.pallas_call(
        _ffn_kernel,
        grid=(t // tm, f // tf),
        in_specs=[
            pl.BlockSpec((tm, d), lambda i, j: (i, 0)),
            pl.BlockSpec((1, d), lambda i, j: (0, 0)),
            pl.BlockSpec((d, tf), lambda i, j: (0, j)),
            pl.BlockSpec((d, tf), lambda i, j: (0, j)),
            pl.BlockSpec((tf, d), lambda i, j: (j, 0)),
        ],
        out_specs=pl.BlockSpec((tm, d), lambda i, j: (i, 0)),
        out_shape=jax.ShapeDtypeStruct((t, d), F32),
        scratch_shapes=[pltpu.VMEM((tm, d), BF16), pltpu.VMEM((tm, d), F32)],
        compiler_params=_params(("parallel", "arbitrary"), nbytes),
        name="ffn_dense",
    )(x, g.reshape(1, d), wg, wu, wd)


def _router_kernel(x_ref, g_ref, wr_ref, tri_ref, xn_ref, mi_ref, mf_ref, cnt_ref, run_ref):
    @pl.when(pl.program_id(0) == 0)
    def _():
        run_ref[...] = jnp.zeros_like(run_ref)

    xn = _rms_scale(x_ref[...], g_ref[...])
    xn_ref[...] = xn
    logits = jnp.dot(xn.astype(BF16), wr_ref[...], preferred_element_type=F32)
    lane = lax.broadcasted_iota(jnp.int32, logits.shape, 1)
    neg = jnp.asarray(-jnp.inf, F32)
    logits = jnp.where(lane < N_EXPERTS, logits, neg)
    m1 = jnp.max(logits, axis=-1, keepdims=True)
    i1 = jnp.min(jnp.where(logits == m1, lane, V7X_LANES), axis=-1, keepdims=True)
    rest = jnp.where(lane == i1, neg, logits)
    m2 = jnp.max(rest, axis=-1, keepdims=True)
    i2 = jnp.min(jnp.where(rest == m2, lane, V7X_LANES), axis=-1, keepdims=True)
    e2 = jnp.exp(m2 - m1)
    denom = 1.0 + e2
    mf_ref[...] = jnp.where(lane == 0, 1.0 / denom, jnp.where(lane == 1, e2 / denom, 0.0))

    chosen = jnp.logical_or(lane == i1, lane == i2)
    onehot = jnp.where(chosen, 1.0, 0.0)
    rank = jnp.dot(tri_ref[...], onehot.astype(BF16), preferred_element_type=F32) + run_ref[...]
    r1 = jnp.sum(jnp.where(lane == i1, rank, 0.0), axis=-1, keepdims=True).astype(jnp.int32)
    r2 = jnp.sum(jnp.where(lane == i2, rank, 0.0), axis=-1, keepdims=True).astype(jnp.int32)
    mi_ref[...] = jnp.where(lane == 0, i1, jnp.where(lane == 1, i2, jnp.where(lane == 2, r1, jnp.where(lane == 3, r2, 0))))
    run_ref[...] += jnp.sum(onehot, axis=0, keepdims=True)
    cnt_ref[...] = run_ref[...]


def router(x, g, w_router_padded):
    t, d = x.shape
    tm = ROW_TILE
    w = V7X_LANES
    r = lax.broadcasted_iota(jnp.int32, (tm, tm), 0)
    c = lax.broadcasted_iota(jnp.int32, (tm, tm), 1)
    tri = (c < r).astype(BF16)
    nbytes = 2 * (2 * tm * d * 4 + d * w * 2 + tm * tm * 2 + 2 * tm * w * 4) + 2 * tm * d * 4
    row_spec = pl.BlockSpec((tm, w), lambda i: (i, 0))
    return pl.pallas_call(
        _router_kernel,
        grid=(t // tm,),
        in_specs=[
            pl.BlockSpec((tm, d), lambda i: (i, 0)),
            pl.BlockSpec((1, d), lambda i: (0, 0)),
            pl.BlockSpec((d, w), lambda i: (0, 0)),
            pl.BlockSpec((tm, tm), lambda i: (0, 0)),
        ],
        out_specs=[pl.BlockSpec((tm, d), lambda i: (i, 0)), row_spec, row_spec, pl.BlockSpec((1, w), lambda i: (0, 0))],
        out_shape=[
            jax.ShapeDtypeStruct((t, d), F32),
            jax.ShapeDtypeStruct((t, w), jnp.int32),
            jax.ShapeDtypeStruct((t, w), F32),
            jax.ShapeDtypeStruct((1, w), F32),
        ],
        scratch_shapes=[pltpu.VMEM((1, w), F32)],
        compiler_params=_params(("arbitrary",), nbytes),
        name="router",
    )(x, g.reshape(1, d), w_router_padded, tri)


def _row_copy(src, src_row, dst, dst_row, sem):
    return pltpu.make_async_copy(src.at[pl.ds(src_row, 1)], dst.at[pl.ds(dst_row, 1)], sem)


def _dispatch_kernel(slots_ref, last_ref, xn_ref, xs_hbm, zero_ref, sem, zero_sem):
    td = xn_ref.shape[0]
    tg = zero_ref.shape[0]

    @pl.when(pl.program_id(0) == 0)
    def _():
        zero_ref[...] = jnp.zeros_like(zero_ref)
        for e in range(last_ref.shape[0]):
            @pl.when(last_ref[e] >= 0)
            def _():
                fill = pltpu.make_async_copy(zero_ref, xs_hbm.at[pl.ds(pl.multiple_of(last_ref[e], tg), tg)], zero_sem)
                fill.start()
                fill.wait()

    def body(r, carry):
        for k in range(2):
            _row_copy(xn_ref, r, xs_hbm, slots_ref[2 * r + k], sem).start(priority=k)
        return carry

    lax.fori_loop(0, td, body, 0, unroll=8)
    for k in range(2):
        pltpu.make_async_copy(xn_ref, xs_hbm.at[pl.ds(0, td)], sem).wait()


def dispatch(slots, last_tile_row, xn, n_slots):
    t, d = xn.shape
    td = MOE_ROUTE_TILE
    tg = MOE_GROUP_TILE
    return pl.pallas_call(
        _dispatch_kernel,
        grid=(t // td,),
        in_specs=[
            pl.BlockSpec((2 * td,), lambda i: (i,), memory_space=pltpu.SMEM),
            pl.BlockSpec(memory_space=pltpu.SMEM),
            pl.BlockSpec((td, d), lambda i: (i, 0)),
        ],
        out_specs=pl.BlockSpec(memory_space=pl.ANY),
        out_shape=jax.ShapeDtypeStruct((n_slots, d), xn.dtype),
        scratch_shapes=[pltpu.VMEM((tg, d), xn.dtype), pltpu.SemaphoreType.DMA, pltpu.SemaphoreType.DMA],
        compiler_params=_params(("arbitrary",), 2 * td * d * 4 + tg * d * 4),
        name="moe_dispatch",
    )(slots, last_tile_row, xn)


def _experts_kernel(te_ref, na_ref, xs_ref, wg_ref, wu_ref, wd_ref, ys_ref, xb_ref, acc_ref):
    del te_ref
    f = pl.program_id(1)
    active = pl.program_id(0) < na_ref[0]

    @pl.when(jnp.logical_and(jnp.logical_not(active), f == 0))
    def _():
        ys_ref[...] = jnp.zeros_like(ys_ref)

    @pl.when(active)
    def _():
        @pl.when(f == 0)
        def _():
            xb_ref[...] = xs_ref[...].astype(BF16)
            acc_ref[...] = jnp.zeros_like(acc_ref)

        xb = xb_ref[...]
        hg = jnp.dot(xb, wg_ref[0], preferred_element_type=F32)
        hu = jnp.dot(xb, wu_ref[0], preferred_element_type=F32)
        act = (jax.nn.silu(hg) * hu).astype(BF16)
        acc_ref[...] += jnp.dot(act, wd_ref[0], preferred_element_type=F32)

        @pl.when(f == pl.num_programs(1) - 1)
        def _():
            ys_ref[...] = acc_ref[...]


def experts(tile_expert, n_active, xs, wg, wu, wd):
    n_slots, d = xs.shape
    fe = wg.shape[2]
    tg = MOE_GROUP_TILE
    tf = MOE_COL_TILE
    nf = fe // tf
    n_tiles = n_slots // tg

    def tile(n, na):
        return jnp.minimum(n, na[0] - 1)

    def col(n, f, na):
        return jnp.where(n < na[0], f, nf - 1)

    nbytes = 2 * (2 * tg * d * 4 + 3 * d * tf * 2) + tg * d * 6 + 4 * tg * tf * 4
    return pl.pallas_call(
        _experts_kernel,
        grid_spec=pltpu.PrefetchScalarGridSpec(
            num_scalar_prefetch=2,
            grid=(n_tiles, nf),
            in_specs=[
                pl.BlockSpec((tg, d), lambda n, f, te, na: (tile(n, na), 0)),
                pl.BlockSpec((1, d, tf), lambda n, f, te, na: (te[tile(n, na)], 0, col(n, f, na))),
                pl.BlockSpec((1, d, tf), lambda n, f, te, na: (te[tile(n, na)], 0, col(n, f, na))),
                pl.BlockSpec((1, tf, d), lambda n, f, te, na: (te[tile(n, na)], col(n, f, na), 0)),
            ],
            out_specs=pl.BlockSpec((tg, d), lambda n, f, te, na: (n, 0)),
            scratch_shapes=[pltpu.VMEM((tg, d), BF16), pltpu.VMEM((tg, d), F32)],
        ),
        out_shape=jax.ShapeDtypeStruct((n_slots, d), F32),
        compiler_params=_params(("arbitrary", "arbitrary"), nbytes),
        name="moe_experts",
    )(tile_expert, n_active, xs, wg, wu, wd)


def _combine_kernel(slots_ref, x_ref, mf_ref, ys_hbm, o_ref, buf_ref, sem):
    tc = x_ref.shape[0]

    def body(r, carry):
        for k in range(2):
            _row_copy(ys_hbm, slots_ref[2 * r + k], buf_ref.at[k], r, sem).start()
        return carry

    lax.fori_loop(0, tc, body, 0, unroll=8)
    for k in range(2):
        pltpu.make_async_copy(ys_hbm.at[pl.ds(0, tc)], buf_ref.at[k], sem).wait()
    lane = lax.broadcasted_iota(jnp.int32, mf_ref.shape, 1)
    mf = mf_ref[...]
    w1 = jnp.sum(jnp.where(lane == 0, mf, 0.0), axis=-1, keepdims=True)
    w2 = jnp.sum(jnp.where(lane == 1, mf, 0.0), axis=-1, keepdims=True)
    o_ref[...] = x_ref[...] + (w1 * buf_ref[0] + w2 * buf_ref[1])


def combine(slots, x, meta_f, ys):
    t, d = x.shape
    tc = MOE_ROUTE_TILE
    w = meta_f.shape[1]
    nbytes = 2 * (2 * tc * d * 4 + tc * w * 4) + 2 * tc * d * 4
    return pl.pallas_call(
        _combine_kernel,
        grid=(t // tc,),
        in_specs=[
            pl.BlockSpec((2 * tc,), lambda i: (i,), memory_space=pltpu.SMEM),
            pl.BlockSpec((tc, d), lambda i: (i, 0)),
            pl.BlockSpec((tc, w), lambda i: (i, 0)),
            pl.BlockSpec(memory_space=pl.ANY),
        ],
        out_specs=pl.BlockSpec((tc, d), lambda i: (i, 0)),
        out_shape=jax.ShapeDtypeStruct((t, d), F32),
        scratch_shapes=[pltpu.VMEM((2, tc, d), F32), pltpu.SemaphoreType.DMA],
        compiler_params=_params(("arbitrary",), nbytes),
        name="moe_combine",
    )(slots, x, meta_f, ys)


def moe_routed(x, g, w_router, wg, wu, wd):
    t, d = x.shape
    w = V7X_LANES
    tg = MOE_GROUP_TILE
    n_tiles = 2 * t // tg + N_EXPERTS
    w_r = jnp.pad(w_router, ((0, 0), (0, w - N_EXPERTS))).astype(BF16)
    xn, meta_i, meta_f, counts = router(x, g, w_r)

    counts = counts[0, :N_EXPERTS].astype(jnp.int32)
    tiles_per = (counts + tg - 1) // tg
    tile_end = jnp.cumsum(tiles_per)
    group_start = (tile_end - tiles_per) * tg
    n_active = tile_end[-1:].astype(jnp.int32)
    tile_ids = jnp.arange(n_tiles, dtype=jnp.int32)
    tile_expert = jnp.minimum(jnp.sum(tile_ids[:, None] >= tile_end[None, :], axis=1), N_EXPERTS - 1).astype(jnp.int32)
    slots = (jnp.take(group_start, meta_i[:, 0:2]) + meta_i[:, 2:4]).reshape(-1).astype(jnp.int32)

    trailing = n_tiles - 1 - jnp.arange(N_EXPERTS, dtype=jnp.int32)
    fill_rows = jnp.concatenate([jnp.where(tiles_per > 0, (tile_end - 1) * tg, -1),
                                 jnp.where(trailing >= n_active[0], trailing * tg, -1)]).astype(jnp.int32)
    xs = dispatch(slots, fill_rows, xn, n_tiles * tg)
    ys = experts(tile_expert, n_active, xs, wg, wu, wd)
    return combine(slots, x, meta_f, ys)


def kernel(x, mix_norm, ffn_norm, mlstm_w_in, mlstm_b_igate, mlstm_b_fgate, mlstm_g_h, mlstm_w_out, kv_norm, w_kv, g_k, sb_w_q, sb_g_q, sb_w_o, ffn_w_gate, ffn_w_up, ffn_w_down, moe_w_router, moe_w_gate, moe_w_up, moe_w_down):
    batch, seq, d = x.shape
    xt = x.reshape(batch * seq, d)
    w = V7X_LANES
    n_main = 2 * MLSTM_HEADS * MLSTM_QK_DIM + 2 * MLSTM_HEADS * MLSTM_V_DIM

    w_in = mlstm_w_in[0]
    w_main = w_in[:, :n_main].astype(BF16)
    w_gate = jnp.pad(w_in[:, n_main:], ((0, 0), (0, w - 2 * MLSTM_HEADS))).astype(BF16)
    bias = jnp.pad(jnp.concatenate([mlstm_b_igate[0], mlstm_b_fgate[0]]), (0, w - 2 * MLSTM_HEADS)).reshape(1, w)
    proj = norm_matmul(xt, mix_norm[0], w_main, BF16, tn=PROJ_COL_TILE)
    graw = norm_matmul(xt, mix_norm[0], w_gate, F32, tn=w)
    igs, bcum = gate_prep(graw, bias)
    hmix = mlstm_core(proj, igs, bcum, mlstm_g_h[0], batch, seq)
    xt = matmul_residual(hmix, mlstm_w_out[0].astype(BF16), xt)

    xt = ffn_dense(xt, ffn_norm[0], ffn_w_gate[0].astype(BF16), ffn_w_up[0].astype(BF16), ffn_w_down[0].astype(BF16))

    sb_width = SB_HEADS * SB_HEAD_DIM
    kt = key_proj(xt, kv_norm, w_kv[:, :sb_width].astype(BF16), g_k, batch, seq)
    v = norm_matmul(xt, kv_norm, w_kv[:, sb_width:].astype(BF16), BF16, tn=PROJ_COL_TILE)
    qraw = norm_matmul(xt, mix_norm[1], sb_w_q[0].astype(BF16), F32, tn=PROJ_COL_TILE)
    o = stick_breaking(qraw, sb_g_q[0], kt, v, batch, seq)
    xt = matmul_residual(o, sb_w_o[0].astype(BF16), xt)

    xt = moe_routed(xt, ffn_norm[1], moe_w_router[0], moe_w_gate[0].astype(BF16), moe_w_up[0].astype(BF16), moe_w_down[0].astype(BF16))
    return xt.reshape(batch, seq, d)
```

```python
import functools

import jax
import jax.numpy as jnp
from jax import lax
from jax.experimental import pallas as pl
from jax.experimental.pallas import tpu as pltpu

F32 = jnp.float32
BF16 = jnp.bfloat16

EPS = 1e-6
GATE_SOFTCAP = 15.0
MLSTM_HEADS = 8
MLSTM_QK_DIM = 64
MLSTM_V_DIM = 128
SB_HEADS = 16
SB_HEAD_DIM = 64
N_EXPERTS = 8

V7X_LANES = 128
V7X_VMEM_BYTES = 64 * 1024 * 1024

ROW_TILE = 512
PROJ_ROW_TILE = 1024
PROJ_COL_TILE = 1024
FFN_ROW_TILE = 1024
FFN_COL_TILE = 256
MOE_COL_TILE = 512
MLSTM_TILE = 256
MOE_GROUP_TILE = 1024
MOE_ROUTE_TILE = 256
SB_Q_TILE = 512
SB_K_TILE = 256


def _vmem_limit(nbytes):
    return int(min(max(2 * nbytes, 32 * 1024 * 1024), V7X_VMEM_BYTES - 8 * 1024 * 1024))


def _params(semantics, nbytes):
    return pltpu.CompilerParams(dimension_semantics=semantics, vmem_limit_bytes=_vmem_limit(nbytes))


def _rms_scale(x, g):
    ms = jnp.mean(x * x, axis=-1, keepdims=True)
    return x * lax.rsqrt(ms + EPS) * g


def _norm_matmul_kernel(x_ref, g_ref, w_ref, o_ref, xn_ref):
    @pl.when(pl.program_id(1) == 0)
    def _():
        xn_ref[...] = _rms_scale(x_ref[...], g_ref[...]).astype(BF16)

    o_ref[...] = jnp.dot(xn_ref[...], w_ref[...], preferred_element_type=F32).astype(o_ref.dtype)


def norm_matmul(x, g, w, out_dtype, tn):
    t, d = x.shape
    n = w.shape[1]
    tm = PROJ_ROW_TILE
    nbytes = 2 * (tm * d * 4 + d * tn * 2 + tm * tn * 4) + tm * d * 2
    return pl.pallas_call(
        _norm_matmul_kernel,
        grid=(t // tm, n // tn),
        in_specs=[
            pl.BlockSpec((tm, d), lambda i, j: (i, 0)),
            pl.BlockSpec((1, d), lambda i, j: (0, 0)),
            pl.BlockSpec((d, tn), lambda i, j: (0, j)),
        ],
        out_specs=pl.BlockSpec((tm, tn), lambda i, j: (i, j)),
        out_shape=jax.ShapeDtypeStruct((t, n), out_dtype),
        scratch_shapes=[pltpu.VMEM((tm, d), BF16)],
        compiler_params=_params(("parallel", "arbitrary"), nbytes),
        name="norm_matmul",
    )(x, g.reshape(1, d), w)


def _matmul_residual_kernel(a_ref, w_ref, r_ref, o_ref):
    o_ref[...] = r_ref[...] + jnp.dot(a_ref[...], w_ref[...], preferred_element_type=F32)


def matmul_residual(a, w, res):
    t, k = a.shape
    n = w.shape[1]
    tm = PROJ_ROW_TILE
    nbytes = 2 * (tm * k * 2 + k * n * 2 + 2 * tm * n * 4)
    return pl.pallas_call(
        _matmul_residual_kernel,
        grid=(t // tm,),
        in_specs=[
            pl.BlockSpec((tm, k), lambda i: (i, 0)),
            pl.BlockSpec((k, n), lambda i: (0, 0)),
            pl.BlockSpec((tm, n), lambda i: (i, 0)),
        ],
        out_specs=pl.BlockSpec((tm, n), lambda i: (i, 0)),
        out_shape=jax.ShapeDtypeStruct((t, n), F32),
        compiler_params=_params(("parallel",), nbytes),
        name="matmul_residual",
    )(a, w, res)


def _split3(x):
    hi = x.astype(BF16)
    r1 = x - hi.astype(F32)
    mid = r1.astype(BF16)
    lo = (r1 - mid.astype(F32)).astype(BF16)
    return hi, mid, lo


def _gate_prep_kernel(g_ref, b_ref, tri_ref, ig_ref, bc_ref):
    z = g_ref[...] + b_ref[...]
    z = GATE_SOFTCAP * jnp.tanh(z / GATE_SOFTCAP)
    ig_ref[...] = z
    lf = jax.nn.log_sigmoid(z)
    hi, mid, lo = _split3(lf)
    parts = jnp.dot(tri_ref[...], jnp.concatenate([hi, mid, lo], axis=1), preferred_element_type=F32)
    w = V7X_LANES
    bc_ref[...] = (parts[:, :w] + parts[:, w:2 * w]) + parts[:, 2 * w:]


def gate_prep(graw, bias):
    t, w = graw.shape
    tl = MLSTM_TILE
    r = lax.broadcasted_iota(jnp.int32, (tl, tl), 0)
    c = lax.broadcasted_iota(jnp.int32, (tl, tl), 1)
    tri = (c <= r).astype(BF16)
    spec = pl.BlockSpec((tl, w), lambda i: (i, 0))
    return pl.pallas_call(
        _gate_prep_kernel,
        grid=(t // tl,),
        in_specs=[spec, pl.BlockSpec((1, w), lambda i: (0, 0)), pl.BlockSpec((tl, tl), lambda i: (0, 0))],
        out_specs=[spec, spec],
        out_shape=[jax.ShapeDtypeStruct((t, w), F32)] * 2,
        compiler_params=_params(("parallel",), 8 * tl * w * 4),
        name="gate_prep",
    )(graw, bias, tri)


def _mlstm_head(head, q_pair, k_pair, v, og, ig, bc, gh, causal, state_ref):
    tl, dv = v.shape
    lane = lax.broadcasted_iota(jnp.int32, (tl, V7X_LANES), 1)
    mine = (lane // MLSTM_QK_DIM) == (head % 2)
    zero = jnp.zeros((), BF16)
    q = jnp.where(mine, q_pair, zero) * jnp.asarray(MLSTM_QK_DIM ** -0.5, BF16)
    k = jnp.where(mine, k_pair, zero)
    assert dv == V7X_LANES
    b_last = bc[tl - 1:tl, :]
    bc = jnp.broadcast_to(bc, (tl, dv))
    ig = jnp.broadcast_to(ig, (tl, dv))

    src = (bc - ig).T[0:1, :]
    decay = jnp.exp(jnp.where(causal, jnp.concatenate([bc] * (tl // dv), axis=1) - src, -jnp.inf))

    s = lax.dot_general(q, k, (((1,), (1,)), ((), ())), preferred_element_type=F32)
    p = (s * decay).astype(BF16)
    v_aug = jnp.concatenate([v, jnp.ones((tl, dv), BF16)], axis=1)
    st = state_ref[...]
    r = jnp.dot(p, v_aug, preferred_element_type=F32)
    eb = jnp.exp(bc)
    r = r + jnp.concatenate([eb, eb], axis=1) * jnp.dot(q, st.astype(BF16), preferred_element_type=F32)
    num = r[:, :dv]
    den = r[:, dv:]
    h = num / jnp.maximum(jnp.abs(den), 1.0)
    h = h * lax.rsqrt(jnp.mean(h * h, axis=-1, keepdims=True) + EPS)
    h = h * gh * jax.nn.sigmoid(og.astype(F32))

    w = jnp.exp(b_last - bc + ig)
    vw = jnp.concatenate([v.astype(F32) * w, w], axis=1).astype(BF16)
    upd = lax.dot_general(k, vw, (((0,), (0,)), ((), ())), preferred_element_type=F32)
    state_ref[...] = jnp.exp(b_last) * st + upd
    return h


def _mlstm_kernel(q_ref, k_ref, v_ref, og_ref, ig_ref, bc_ref, gh_ref, o_ref, state_ref):
    tl = q_ref.shape[0]
    dv = MLSTM_V_DIM
    w = V7X_LANES

    @pl.when(pl.program_id(1) == 0)
    def _():
        state_ref[...] = jnp.zeros_like(state_ref)

    row = lax.broadcasted_iota(jnp.int32, (tl, tl), 0)
    col = lax.broadcasted_iota(jnp.int32, (tl, tl), 1)
    causal = col <= row
    for head in range(MLSTM_HEADS):
        pair = slice((head // 2) * w, (head // 2 + 1) * w)
        mine = slice(head * dv, (head + 1) * dv)
        h = _mlstm_head(head, q_ref[:, pair], k_ref[:, pair], v_ref[:, mine], og_ref[:, mine],
                        ig_ref[:, head:head + 1], bc_ref[:, MLSTM_HEADS + head:MLSTM_HEADS + head + 1],
                        gh_ref[:, mine], causal, state_ref.at[head])
        o_ref[:, mine] = h.astype(o_ref.dtype)


def mlstm_core(proj, igs, bcum, g_h, batch, seq):
    t = proj.shape[0]
    tl = MLSTM_TILE
    nt = seq // tl
    heads = MLSTM_HEADS
    dv = MLSTM_V_DIM
    w = V7X_LANES
    qk = heads * MLSTM_QK_DIM
    hv = heads * dv

    def rows(b, i):
        return b * nt + i

    nbytes = 2 * (2 * tl * qk * 2 + 3 * tl * hv * 2 + 2 * tl * w * 4) + heads * w * 2 * dv * 4 + 16 * tl * tl * 4
    return pl.pallas_call(
        _mlstm_kernel,
        grid=(batch, nt),
        in_specs=[
            pl.BlockSpec((tl, qk), lambda b, i: (rows(b, i), 0)),
            pl.BlockSpec((tl, qk), lambda b, i: (rows(b, i), 1)),
            pl.BlockSpec((tl, hv), lambda b, i: (rows(b, i), 2 * qk // hv)),
            pl.BlockSpec((tl, hv), lambda b, i: (rows(b, i), 2 * qk // hv + 1)),
            pl.BlockSpec((tl, w), lambda b, i: (rows(b, i), 0)),
            pl.BlockSpec((tl, w), lambda b, i: (rows(b, i), 0)),
            pl.BlockSpec((1, hv), lambda b, i: (0, 0)),
        ],
        out_specs=pl.BlockSpec((tl, hv), lambda b, i: (rows(b, i), 0)),
        out_shape=jax.ShapeDtypeStruct((t, hv), BF16),
        scratch_shapes=[pltpu.VMEM((heads, w, 2 * dv), F32)],
        compiler_params=_params(("parallel", "arbitrary"), nbytes),
        name="mlstm_core",
    )(proj, proj, proj, proj, igs, bcum, g_h.reshape(1, hv))


def _key_proj_kernel(x_ref, g_ref, w_ref, gk_ref, o_ref, xn_ref):
    @pl.when(pl.program_id(1) == 0)
    def _():
        xn_ref[...] = _rms_scale(x_ref[...], g_ref[...]).astype(BF16)

    y = jnp.dot(xn_ref[...], w_ref[...], preferred_element_type=F32)
    d = SB_HEAD_DIM
    for c in range(y.shape[1] // V7X_LANES):
        yt = y[:, c * V7X_LANES:(c + 1) * V7X_LANES].T
        for half in range(V7X_LANES // d):
            blk = yt[half * d:(half + 1) * d, :]
            ms = jnp.mean(blk * blk, axis=0, keepdims=True)
            kn = blk * lax.rsqrt(ms + EPS) * gk_ref[...]
            o_ref[0, c, half * d:(half + 1) * d, :] = kn.astype(o_ref.dtype)


def key_proj(x, g, w, g_k, batch, seq):
    t, d = x.shape
    n = w.shape[1]
    tm = ROW_TILE
    tn = 512
    ns = seq // tm
    pairs = n // V7X_LANES
    nbytes = 2 * (tm * d * 4 + d * tn * 2 + tm * tn * 2) + tm * d * 2 + 2 * tm * tn * 4
    return pl.pallas_call(
        _key_proj_kernel,
        grid=(t // tm, n // tn),
        in_specs=[
            pl.BlockSpec((tm, d), lambda i, j: (i, 0)),
            pl.BlockSpec((1, d), lambda i, j: (0, 0)),
            pl.BlockSpec((d, tn), lambda i, j: (0, j)),
            pl.BlockSpec((SB_HEAD_DIM, 1), lambda i, j: (0, 0)),
        ],
        out_specs=pl.BlockSpec((1, tn // V7X_LANES, V7X_LANES, tm), lambda i, j: (i // ns, j, 0, i % ns)),
        out_shape=jax.ShapeDtypeStruct((batch, pairs, V7X_LANES, seq), BF16),
        scratch_shapes=[pltpu.VMEM((tm, d), BF16)],
        compiler_params=_params(("parallel", "arbitrary"), nbytes),
        name="key_proj",
    )(x, g.reshape(1, d), w, g_k.reshape(SB_HEAD_DIM, 1))


LOG2E = 1.4426950408889634


def _sb_scores(qh, kt, z_ref):
    z_ref[...] = jnp.dot(qh, kt, preferred_element_type=F32)


def _sb_sticks(z_ref, upper, run, first_valid_col, zs_ref, bt_ref):
    tq, tk = z_ref.shape
    z = z_ref[...]
    sp = jnp.maximum(z, 0.0) + jnp.log(1.0 + jnp.exp2(-jnp.abs(z))) * LOG2E
    zs = (z - sp) - run
    if first_valid_col is not None:
        valid = lax.broadcasted_iota(jnp.int32, (tq, tk), 1) < first_valid_col
        sp = jnp.where(valid, sp, 0.0)
        zs = jnp.where(valid, zs, -jnp.inf)
    zs_ref[...] = zs
    bt_ref[...] = jnp.dot(sp.astype(BF16), upper, preferred_element_type=F32)
    return run + jnp.sum(sp, axis=-1, keepdims=True)


def _sb_values(zs_ref, bt_ref, vb, acc_ref):
    a = jnp.exp2(zs_ref[...] - bt_ref[...])
    acc_ref[...] += jnp.dot(a.astype(BF16), vb, preferred_element_type=F32)


def _sb_kernel(q_ref, gq_ref, kt_ref, v_ref, up_ref, o_ref, z_ref, zs_ref, bt_ref, acc_ref):
    i = pl.program_id(2)
    tq = q_ref.shape[0]
    tk = up_ref.shape[0]
    assert tq == 2 * tk, "the schedule below visits the key blocks of a query tile in pairs"
    d = SB_HEAD_DIM
    x = q_ref[...]
    lane = lax.broadcasted_iota(jnp.int32, x.shape, 1)
    first = lane < d
    x2 = x * x
    ss_a = jnp.sum(jnp.where(first, x2, 0.0), axis=-1, keepdims=True)
    ss_b = jnp.sum(jnp.where(first, 0.0, x2), axis=-1, keepdims=True)
    inv = jnp.where(first, lax.rsqrt(ss_a / d + EPS), lax.rsqrt(ss_b / d + EPS))
    qn = x * inv * gq_ref[...] * (LOG2E * d ** -0.5)
    heads = (jnp.where(first, qn, 0.0).astype(BF16), jnp.where(first, 0.0, qn).astype(BF16))
    upper = up_ref[...]
    row = lax.broadcasted_iota(jnp.int32, (tq, 1), 0)
    n_blocks = 2 * i + 2

    def start(n):
        kb = jnp.maximum(n_blocks - 1 - n, 0)
        return pl.multiple_of(kb * tk, tk)

    everything = pl.ds(0, tq)
    second_half = pl.ds(tk, tk)

    def scores(n, slot, rows=everything):
        kt = kt_ref[0, 0, :, pl.ds(start(n), tk)]
        for h in range(2):
            _sb_scores(heads[h][rows.start:rows.start + rows.size], kt, z_ref.at[slot, h, rows])

    def sticks(slot, runs, first_valid_col, rows=everything):
        return tuple(_sb_sticks(z_ref.at[slot, h, rows], upper, runs[h], first_valid_col, zs_ref.at[slot, h, rows],
                                bt_ref.at[slot, h, rows]) for h in range(2))

    def values(n, slot, rows=everything):
        vb = v_ref[pl.ds(start(n), tk), :]
        for h in range(2):
            _sb_values(zs_ref.at[slot, h, rows], bt_ref.at[slot, h, rows], vb, acc_ref.at[h, rows])

    acc_ref[...] = jnp.zeros_like(acc_ref)
    half_zero = jnp.zeros((tk, 1), F32)
    scores(0, 0, second_half)
    scores(1, 1)
    runs = sticks(0, (half_zero, half_zero), row[:tk], second_half)
    runs = tuple(jnp.concatenate([half_zero, r], axis=0) for r in runs)
    scores(2, 0)
    runs = sticks(1, runs, row)
    values(0, 0, second_half)

    def body(m, runs):
        n = 2 + 2 * m
        scores(n + 1, 1)
        runs = sticks(0, runs, None)
        values(n - 1, 1)
        scores(n + 2, 0)
        runs = sticks(1, runs, None)
        values(n, 0)
        return runs

    lax.fori_loop(0, i, body, runs)
    values(n_blocks - 1, 1)
    o_ref[...] = jnp.where(first, acc_ref[0], acc_ref[1]).astype(o_ref.dtype)


def stick_breaking(qraw, g_q, kt, v, batch, seq):
    t, width = qraw.shape
    tq = SB_Q_TILE
    tk = SB_K_TILE
    nq = seq // tq
    w = V7X_LANES
    pairs = width // w
    r = lax.broadcasted_iota(jnp.int32, (tk, tk), 0)
    c = lax.broadcasted_iota(jnp.int32, (tk, tk), 1)
    upper = (r > c).astype(BF16)
    gq2 = jnp.tile(g_q.reshape(1, SB_HEAD_DIM), (1, w // SB_HEAD_DIM))
    nbytes = 2 * (tq * w * 4 + 2 * seq * w * 2 + tk * tk * 2 + tq * w * 2) + 32 * tq * tk * 4
    return pl.pallas_call(
        _sb_kernel,
        grid=(batch, pairs, nq),
        in_specs=[
            pl.BlockSpec((tq, w), lambda b, p, i: (b * nq + i, p)),
            pl.BlockSpec((1, w), lambda b, p, i: (0, 0)),
            pl.BlockSpec((1, 1, w, seq), lambda b, p, i: (b, p, 0, 0)),
            pl.BlockSpec((seq, w), lambda b, p, i: (b, p)),
            pl.BlockSpec((tk, tk), lambda b, p, i: (0, 0)),
        ],
        out_specs=pl.BlockSpec((tq, w), lambda b, p, i: (b * nq + i, p)),
        out_shape=jax.ShapeDtypeStruct((t, width), BF16),
        scratch_shapes=[
            pltpu.VMEM((2, 2, tq, tk), F32),
            pltpu.VMEM((2, 2, tq, tk), F32),
            pltpu.VMEM((2, 2, tq, tk), F32),
            pltpu.VMEM((2, tq, w), F32),
        ],
        compiler_params=_params(("parallel", "parallel", "arbitrary"), nbytes),
        name="stick_breaking",
    )(qraw, gq2, kt, v, upper)


def _ffn_kernel(x_ref, g_ref, wg_ref, wu_ref, wd_ref, o_ref, xn_ref, acc_ref):
    f = pl.program_id(1)

    @pl.when(f == 0)
    def _():
        xn_ref[...] = _rms_scale(x_ref[...], g_ref[...]).astype(BF16)
        acc_ref[...] = jnp.zeros_like(acc_ref)

    xn = xn_ref[...]
    hg = jnp.dot(xn, wg_ref[...], preferred_element_type=F32)
    hu = jnp.dot(xn, wu_ref[...], preferred_element_type=F32)
    act = (jax.nn.silu(hg) * hu).astype(BF16)
    acc_ref[...] += jnp.dot(act, wd_ref[...], preferred_element_type=F32)

    @pl.when(f == pl.num_programs(1) - 1)
    def _():
        o_ref[...] = x_ref[...] + acc_ref[...]


def ffn_dense(x, g, wg, wu, wd):
    t, d = x.shape
    f = wg.shape[1]
    tm = FFN_ROW_TILE
    tf = FFN_COL_TILE
    nbytes = 2 * (2 * tm * d * 4 + 3 * d * tf * 2) + tm * d * 6 + 4 * tm * tf * 4
    return pl.pallas_call(
        _ffn_kernel,
        grid=(t // tm, f // tf),
        in_specs=[
            pl.BlockSpec((tm, d), lambda i, j: (i, 0)),
            pl.BlockSpec((1, d), lambda i, j: (0, 0)),
            pl.BlockSpec((d, tf), lambda i, j: (0, j)),
            pl.BlockSpec((d, tf), lambda i, j: (0, j)),
            pl.BlockSpec((tf, d), lambda i, j: (j, 0)),
        ],
        out_specs=pl.BlockSpec((tm, d), lambda i, j: (i, 0)),
        out_shape=jax.ShapeDtypeStruct((t, d), F32),
        scratch_shapes=[pltpu.VMEM((tm, d), BF16), pltpu.VMEM((tm, d), F32)],
        compiler_params=_params(("parallel", "arbitrary"), nbytes),
        name="ffn_dense",
    )(x, g.reshape(1, d), wg, wu, wd)


def _router_kernel(x_ref, g_ref, wr_ref, tri_ref, xn_ref, mi_ref, mf_ref, cnt_ref, run_ref):
    @pl.when(pl.program_id(0) == 0)
    def _():
        run_ref[...] = jnp.zeros_like(run_ref)

    xn = _rms_scale(x_ref[...], g_ref[...])
    xn_ref[...] = xn
    logits = jnp.dot(xn.astype(BF16), wr_ref[...], preferred_element_type=F32)
    lane = lax.broadcasted_iota(jnp.int32, logits.shape, 1)
    neg = jnp.asarray(-jnp.inf, F32)
    logits = jnp.where(lane < N_EXPERTS, logits, neg)
    m1 = jnp.max(logits, axis=-1, keepdims=True)
    i1 = jnp.min(jnp.where(logits == m1, lane, V7X_LANES), axis=-1, keepdims=True)
    rest = jnp.where(lane == i1, neg, logits)
    m2 = jnp.max(rest, axis=-1, keepdims=True)
    i2 = jnp.min(jnp.where(rest == m2, lane, V7X_LANES), axis=-1, keepdims=True)
    e2 = jnp.exp(m2 - m1)
    denom = 1.0 + e2
    mf_ref[...] = jnp.where(lane == 0, 1.0 / denom, jnp.where(lane == 1, e2 / denom, 0.0))

    chosen = jnp.logical_or(lane == i1, lane == i2)
    onehot = jnp.where(chosen, 1.0, 0.0)
    rank = jnp.dot(tri_ref[...], onehot.astype(BF16), preferred_element_type=F32) + run_ref[...]
    r1 = jnp.sum(jnp.where(lane == i1, rank, 0.0), axis=-1, keepdims=True).astype(jnp.int32)
    r2 = jnp.sum(jnp.where(lane == i2, rank, 0.0), axis=-1, keepdims=True).astype(jnp.int32)
    mi_ref[...] = jnp.where(lane == 0, i1, jnp.where(lane == 1, i2, jnp.where(lane == 2, r1, jnp.where(lane == 3, r2, 0))))
    run_ref[...] += jnp.sum(onehot, axis=0, keepdims=True)
    cnt_ref[...] = run_ref[...]


def router(x, g, w_router_padded):
    t, d = x.shape
    tm = ROW_TILE
    w = V7X_LANES
    r = lax.broadcasted_iota(jnp.int32, (tm, tm), 0)
    c = lax.broadcasted_iota(jnp.int32, (tm, tm), 1)
    tri = (c < r).astype(BF16)
    nbytes = 2 * (2 * tm * d * 4 + d * w * 2 + tm * tm * 2 + 2 * tm * w * 4) + 2 * tm * d * 4
    row_spec = pl.BlockSpec((tm, w), lambda i: (i, 0))
    return pl.pallas_call(
        _router_kernel,
        grid=(t // tm,),
        in_specs=[
            pl.BlockSpec((tm, d), lambda i: (i, 0)),
            pl.BlockSpec((1, d), lambda i: (0, 0)),
            pl.BlockSpec((d, w), lambda i: (0, 0)),
            pl.BlockSpec((tm, tm), lambda i: (0, 0)),
        ],
        out_specs=[pl.BlockSpec((tm, d), lambda i: (i, 0)), row_spec, row_spec, pl.BlockSpec((1, w), lambda i: (0, 0))],
        out_shape=[
            jax.ShapeDtypeStruct((t, d), F32),
            jax.ShapeDtypeStruct((t, w), jnp.int32),
            jax.ShapeDtypeStruct((t, w), F32),
            jax.ShapeDtypeStruct((1, w), F32),
        ],
        scratch_shapes=[pltpu.VMEM((1, w), F32)],
        compiler_params=_params(("arbitrary",), nbytes),
        name="router",
    )(x, g.reshape(1, d), w_router_padded, tri)


def _row_copy(src, src_row, dst, dst_row, sem):
    return pltpu.make_async_copy(src.at[pl.ds(src_row, 1)], dst.at[pl.ds(dst_row, 1)], sem)


def _dispatch_kernel(slots_ref, last_ref, xn_ref, xs_hbm, zero_ref, sem, zero_sem):
    td = xn_ref.shape[0]
    tg = zero_ref.shape[0]

    @pl.when(pl.program_id(0) == 0)
    def _():
        zero_ref[...] = jnp.zeros_like(zero_ref)

        def fill(e):
            return pltpu.make_async_copy(zero_ref, xs_hbm.at[pl.ds(pl.multiple_of(last_ref[e], tg), tg)], zero_sem)

        for e in range(last_ref.shape[0]):
            @pl.when(last_ref[e] >= 0)
            def _(e=e):
                fill(e).start()

        for e in range(last_ref.shape[0]):
            @pl.when(last_ref[e] >= 0)
            def _(e=e):
                fill(e).wait()

    def body(r, carry):
        for k in range(2):
            _row_copy(xn_ref, r, xs_hbm, slots_ref[2 * r + k], sem).start(priority=k)
        return carry

    lax.fori_loop(0, td, body, 0, unroll=8)
    for k in range(2):
        pltpu.make_async_copy(xn_ref, xs_hbm.at[pl.ds(0, td)], sem).wait()


def dispatch(slots, last_tile_row, xn, n_slots):
    t, d = xn.shape
    td = MOE_ROUTE_TILE
    tg = MOE_GROUP_TILE
    return pl.pallas_call(
        _dispatch_kernel,
        grid=(t // td,),
        in_specs=[
            pl.BlockSpec((2 * td,), lambda i: (i,), memory_space=pltpu.SMEM),
            pl.BlockSpec(memory_space=pltpu.SMEM),
            pl.BlockSpec((td, d), lambda i: (i, 0)),
        ],
        out_specs=pl.BlockSpec(memory_space=pl.ANY),
        out_shape=jax.ShapeDtypeStruct((n_slots, d), xn.dtype),
        scratch_shapes=[pltpu.VMEM((tg, d), xn.dtype), pltpu.SemaphoreType.DMA, pltpu.SemaphoreType.DMA],
        compiler_params=_params(("arbitrary",), 2 * td * d * 4 + tg * d * 4),
        name="moe_dispatch",
    )(slots, last_tile_row, xn)


def _experts_kernel(te_ref, na_ref, xs_ref, wg_ref, wu_ref, wd_ref, ys_ref, xb_ref, acc_ref):
    del te_ref
    f = pl.program_id(1)
    active = pl.program_id(0) < na_ref[0]

    @pl.when(jnp.logical_and(jnp.logical_not(active), f == 0))
    def _():
        ys_ref[...] = jnp.zeros_like(ys_ref)

    @pl.when(active)
    def _():
        @pl.when(f == 0)
        def _():
            xb_ref[...] = xs_ref[...].astype(BF16)
            acc_ref[...] = jnp.zeros_like(acc_ref)

        xb = xb_ref[...]
        hg = jnp.dot(xb, wg_ref[0], preferred_element_type=F32)
        hu = jnp.dot(xb, wu_ref[0], preferred_element_type=F32)
        act = (jax.nn.silu(hg) * hu).astype(BF16)
        acc_ref[...] += jnp.dot(act, wd_ref[0], preferred_element_type=F32)

        @pl.when(f == pl.num_programs(1) - 1)
        def _():
            ys_ref[...] = acc_ref[...]


def experts(tile_expert, n_active, xs, wg, wu, wd):
    n_slots, d = xs.shape
    fe = wg.shape[2]
    tg = MOE_GROUP_TILE
    tf = MOE_COL_TILE
    nf = fe // tf
    n_tiles = n_slots // tg

    def tile(n, na):
        return jnp.minimum(n, na[0] - 1)

    def col(n, f, na):
        return jnp.where(n < na[0], f, nf - 1)

    nbytes = 2 * (2 * tg * d * 4 + 3 * d * tf * 2) + tg * d * 6 + 4 * tg * tf * 4
    return pl.pallas_call(
        _experts_kernel,
        grid_spec=pltpu.PrefetchScalarGridSpec(
            num_scalar_prefetch=2,
            grid=(n_tiles, nf),
            in_specs=[
                pl.BlockSpec((tg, d), lambda n, f, te, na: (tile(n, na), 0)),
                pl.BlockSpec((1, d, tf), lambda n, f, te, na: (te[tile(n, na)], 0, col(n, f, na))),
                pl.BlockSpec((1, d, tf), lambda n, f, te, na: (te[tile(n, na)], 0, col(n, f, na))),
                pl.BlockSpec((1, tf, d), lambda n, f, te, na: (te[tile(n, na)], col(n, f, na), 0)),
            ],
            out_specs=pl.BlockSpec((tg, d), lambda n, f, te, na: (n, 0)),
            scratch_shapes=[pltpu.VMEM((tg, d), BF16), pltpu.VMEM((tg, d), F32)],
        ),
        out_shape=jax.ShapeDtypeStruct((n_slots, d), F32),
        compiler_params=_params(("arbitrary", "arbitrary"), nbytes),
        name="moe_experts",
    )(tile_expert, n_active, xs, wg, wu, wd)


def _combine_kernel(slots_ref, next_slots_ref, x_ref, mf_ref, ys_hbm, o_ref, buf_ref, sems):
    step = pl.program_id(0)
    tc = x_ref.shape[0]

    def gather(s_ref, parity):
        def body(r, carry):
            for k in range(2):
                _row_copy(ys_hbm, s_ref[2 * r + k], buf_ref.at[parity, k], r, sems.at[parity]).start()
            return carry

        lax.fori_loop(0, tc, body, 0, unroll=8)

    @pl.when(step == 0)
    def _():
        gather(slots_ref, 0)

    lane = lax.broadcasted_iota(jnp.int32, mf_ref.shape, 1)
    mf = mf_ref[...]
    w1 = jnp.sum(jnp.where(lane == 0, mf, 0.0), axis=-1, keepdims=True)
    w2 = jnp.sum(jnp.where(lane == 1, mf, 0.0), axis=-1, keepdims=True)
    for parity in range(2):
        @pl.when(step % 2 == parity)
        def _(parity=parity):
            @pl.when(step + 1 < pl.num_programs(0))
            def _():
                gather(next_slots_ref, 1 - parity)

            for k in range(2):
                pltpu.make_async_copy(ys_hbm.at[pl.ds(0, tc)], buf_ref.at[parity, k], sems.at[parity]).wait()
            o_ref[...] = x_ref[...] + (w1 * buf_ref[parity, 0] + w2 * buf_ref[parity, 1])


def combine(slots, x, meta_f, ys):
    t, d = x.shape
    tc = MOE_ROUTE_TILE
    w = meta_f.shape[1]
    steps = t // tc
    nbytes = 2 * (2 * tc * d * 4 + tc * w * 4) + 4 * tc * d * 4
    return pl.pallas_call(
        _combine_kernel,
        grid=(steps,),
        in_specs=[
            pl.BlockSpec((2 * tc,), lambda i: (i,), memory_space=pltpu.SMEM),
            pl.BlockSpec((2 * tc,), lambda i: (jnp.minimum(i + 1, steps - 1),), memory_space=pltpu.SMEM),
            pl.BlockSpec((tc, d), lambda i: (i, 0)),
            pl.BlockSpec((tc, w), lambda i: (i, 0)),
            pl.BlockSpec(memory_space=pl.ANY),
        ],
        out_specs=pl.BlockSpec((tc, d), lambda i: (i, 0)),
        out_shape=jax.ShapeDtypeStruct((t, d), F32),
        scratch_shapes=[pltpu.VMEM((2, 2, tc, d), F32), pltpu.SemaphoreType.DMA((2,))],
        compiler_params=_params(("arbitrary",), nbytes),
        name="moe_combine",
    )(slots, slots, x, meta_f, ys)


def moe_routed(x, g, w_router, wg, wu, wd):
    t, d = x.shape
    w = V7X_LANES
    tg = MOE_GROUP_TILE
    n_tiles = 2 * t // tg + N_EXPERTS
    w_r = jnp.pad(w_router, ((0, 0), (0, w - N_EXPERTS))).astype(BF16)
    xn, meta_i, meta_f, counts = router(x, g, w_r)

    counts = counts[0, :N_EXPERTS].astype(jnp.int32)
    tiles_per = (counts + tg - 1) // tg
    tile_end = jnp.cumsum(tiles_per)
    group_start = (tile_end - tiles_per) * tg
    n_active = tile_end[-1:].astype(jnp.int32)
    tile_ids = jnp.arange(n_tiles, dtype=jnp.int32)
    tile_expert = jnp.minimum(jnp.sum(tile_ids[:, None] >= tile_end[None, :], axis=1), N_EXPERTS - 1).astype(jnp.int32)
    slots = (jnp.take(group_start, meta_i[:, 0:2]) + meta_i[:, 2:4]).reshape(-1).astype(jnp.int32)

    trailing = n_tiles - 1 - jnp.arange(N_EXPERTS, dtype=jnp.int32)
    fill_rows = jnp.concatenate([jnp.where(tiles_per > 0, (tile_end - 1) * tg, -1),
                                 jnp.where(trailing >= n_active[0], trailing * tg, -1)]).astype(jnp.int32)
    xs = dispatch(slots, fill_rows, xn, n_tiles * tg)
    ys = experts(tile_expert, n_active, xs, wg, wu, wd)
    return combine(slots, x, meta_f, ys)


def kernel(x, mix_norm, ffn_norm, mlstm_w_in, mlstm_b_igate, mlstm_b_fgate, mlstm_g_h, mlstm_w_out, kv_norm, w_kv, g_k, sb_w_q, sb_g_q, sb_w_o, ffn_w_gate, ffn_w_up, ffn_w_down, moe_w_router, moe_w_gate, moe_w_up, moe_w_down):
    batch, seq, d = x.shape
    xt = x.reshape(batch * seq, d)
    w = V7X_LANES
    n_main = 2 * MLSTM_HEADS * MLSTM_QK_DIM + 2 * MLSTM_HEADS * MLSTM_V_DIM

    w_in = mlstm_w_in[0]
    w_main = w_in[:, :n_main].astype(BF16)
    w_gate = jnp.pad(w_in[:, n_main:], ((0, 0), (0, w - 2 * MLSTM_HEADS))).astype(BF16)
    bias = jnp.pad(jnp.concatenate([mlstm_b_igate[0], mlstm_b_fgate[0]]), (0, w - 2 * MLSTM_HEADS)).reshape(1, w)
    proj = norm_matmul(xt, mix_norm[0], w_main, BF16, tn=PROJ_COL_TILE)
    graw = norm_matmul(xt, mix_norm[0], w_gate, F32, tn=w)
    igs, bcum = gate_prep(graw, bias)
    hmix = mlstm_core(proj, igs, bcum, mlstm_g_h[0], batch, seq)
    xt = matmul_residual(hmix, mlstm_w_out[0].astype(BF16), xt)

    xt = ffn_dense(xt, ffn_norm[0], ffn_w_gate[0].astype(BF16), ffn_w_up[0].astype(BF16), ffn_w_down[0].astype(BF16))

    sb_width = SB_HEADS * SB_HEAD_DIM
    kt = key_proj(xt, kv_norm, w_kv[:, :sb_width].astype(BF16), g_k, batch, seq)
    v = norm_matmul(xt, kv_norm, w_kv[:, sb_width:].astype(BF16), BF16, tn=PROJ_COL_TILE)
    qraw = norm_matmul(xt, mix_norm[1], sb_w_q[0].astype(BF16), F32, tn=PROJ_COL_TILE)
    o = stick_breaking(qraw, sb_g_q[0], kt, v, batch, seq)
    xt = matmul_residual(o, sb_w_o[0].astype(BF16), xt)

    xt = moe_routed(xt, ffn_norm[1], moe_w_router[0], moe_w_gate[0].astype(BF16), moe_w_up[0].astype(BF16), moe_w_down[0].astype(BF16))
    return xt.reshape(batch, seq, d)
```

```python
import functools

import jax
import jax.numpy as jnp
from jax import lax
from jax.experimental import pallas as pl
from jax.experimental.pallas import tpu as pltpu

F32 = jnp.float32
BF16 = jnp.bfloat16

EPS = 1e-6
GATE_SOFTCAP = 15.0
MLSTM_HEADS = 8
MLSTM_QK_DIM = 64
MLSTM_V_DIM = 128
SB_HEADS = 16
SB_HEAD_DIM = 64
N_EXPERTS = 8

V7X_LANES = 128
V7X_VMEM_BYTES = 64 * 1024 * 1024

ROW_TILE = 512
PROJ_ROW_TILE = 1024
PROJ_COL_TILE = 1024
FFN_ROW_TILE = 1024
FFN_COL_TILE = 256
MOE_COL_TILE = 512
MLSTM_TILE = 256
MOE_GROUP_TILE = 1024
MOE_ROUTE_TILE = 256
SB_Q_TILE = 512
SB_K_TILE = 256


def _vmem_limit(nbytes):
    return int(min(max(2 * nbytes, 32 * 1024 * 1024), V7X_VMEM_BYTES - 8 * 1024 * 1024))


def _params(semantics, nbytes):
    return pltpu.CompilerParams(dimension_semantics=semantics, vmem_limit_bytes=_vmem_limit(nbytes))


def _rms_scale(x, g):
    ms = jnp.mean(x * x, axis=-1, keepdims=True)
    return x * lax.rsqrt(ms + EPS) * g


def _norm_matmul_kernel(x_ref, g_ref, w_ref, o_ref, xn_ref):
    @pl.when(pl.program_id(1) == 0)
    def _():
        xn_ref[...] = _rms_scale(x_ref[...], g_ref[...]).astype(BF16)

    o_ref[...] = jnp.dot(xn_ref[...], w_ref[...], preferred_element_type=F32).astype(o_ref.dtype)


def norm_matmul(x, g, w, out_dtype, tn):
    t, d = x.shape
    n = w.shape[1]
    tm = PROJ_ROW_TILE
    nbytes = 2 * (tm * d * 4 + d * tn * 2 + tm * tn * 4) + tm * d * 2
    return pl.pallas_call(
        _norm_matmul_kernel,
        grid=(t // tm, n // tn),
        in_specs=[
            pl.BlockSpec((tm, d), lambda i, j: (i, 0)),
            pl.BlockSpec((1, d), lambda i, j: (0, 0)),
            pl.BlockSpec((d, tn), lambda i, j: (0, j)),
        ],
        out_specs=pl.BlockSpec((tm, tn), lambda i, j: (i, j)),
        out_shape=jax.ShapeDtypeStruct((t, n), out_dtype),
        scratch_shapes=[pltpu.VMEM((tm, d), BF16)],
        compiler_params=_params(("parallel", "arbitrary"), nbytes),
        name="norm_matmul",
    )(x, g.reshape(1, d), w)


def _matmul_residual_kernel(a_ref, w_ref, r_ref, o_ref):
    o_ref[...] = r_ref[...] + jnp.dot(a_ref[...], w_ref[...], preferred_element_type=F32)


def matmul_residual(a, w, res):
    t, k = a.shape
    n = w.shape[1]
    tm = PROJ_ROW_TILE
    nbytes = 2 * (tm * k * 2 + k * n * 2 + 2 * tm * n * 4)
    return pl.pallas_call(
        _matmul_residual_kernel,
        grid=(t // tm,),
        in_specs=[
            pl.BlockSpec((tm, k), lambda i: (i, 0)),
            pl.BlockSpec((k, n), lambda i: (0, 0)),
            pl.BlockSpec((tm, n), lambda i: (i, 0)),
        ],
        out_specs=pl.BlockSpec((tm, n), lambda i: (i, 0)),
        out_shape=jax.ShapeDtypeStruct((t, n), F32),
        compiler_params=_params(("parallel",), nbytes),
        name="matmul_residual",
    )(a, w, res)


def _split3(x):
    hi = x.astype(BF16)
    r1 = x - hi.astype(F32)
    mid = r1.astype(BF16)
    lo = (r1 - mid.astype(F32)).astype(BF16)
    return hi, mid, lo


def _gate_prep_kernel(g_ref, b_ref, tri_ref, ig_ref, bc_ref):
    z = g_ref[...] + b_ref[...]
    z = GATE_SOFTCAP * jnp.tanh(z / GATE_SOFTCAP)
    ig_ref[...] = z
    lf = jax.nn.log_sigmoid(z)
    hi, mid, lo = _split3(lf)
    parts = jnp.dot(tri_ref[...], jnp.concatenate([hi, mid, lo], axis=1), preferred_element_type=F32)
    w = V7X_LANES
    bc_ref[...] = (parts[:, :w] + parts[:, w:2 * w]) + parts[:, 2 * w:]


def gate_prep(graw, bias):
    t, w = graw.shape
    tl = MLSTM_TILE
    r = lax.broadcasted_iota(jnp.int32, (tl, tl), 0)
    c = lax.broadcasted_iota(jnp.int32, (tl, tl), 1)
    tri = (c <= r).astype(BF16)
    spec = pl.BlockSpec((tl, w), lambda i: (i, 0))
    return pl.pallas_call(
        _gate_prep_kernel,
        grid=(t // tl,),
        in_specs=[spec, pl.BlockSpec((1, w), lambda i: (0, 0)), pl.BlockSpec((tl, tl), lambda i: (0, 0))],
        out_specs=[spec, spec],
        out_shape=[jax.ShapeDtypeStruct((t, w), F32)] * 2,
        compiler_params=_params(("parallel",), 8 * tl * w * 4),
        name="gate_prep",
    )(graw, bias, tri)


def _mlstm_head(head, q_pair, k_pair, v, og, ig, bc, gh, causal, state_ref):
    tl, dv = v.shape
    lane = lax.broadcasted_iota(jnp.int32, (tl, V7X_LANES), 1)
    mine = (lane // MLSTM_QK_DIM) == (head % 2)
    zero = jnp.zeros((), BF16)
    q = jnp.where(mine, q_pair, zero) * jnp.asarray(MLSTM_QK_DIM ** -0.5, BF16)
    k = jnp.where(mine, k_pair, zero)
    assert dv == V7X_LANES
    b_last = bc[tl - 1:tl, :]
    bc = jnp.broadcast_to(bc, (tl, dv))
    ig = jnp.broadcast_to(ig, (tl, dv))

    src = (bc - ig).T[0:1, :]
    decay = jnp.exp(jnp.where(causal, jnp.concatenate([bc] * (tl // dv), axis=1) - src, -jnp.inf))

    s = lax.dot_general(q, k, (((1,), (1,)), ((), ())), preferred_element_type=F32)
    p = (s * decay).astype(BF16)
    v_aug = jnp.concatenate([v, jnp.ones((tl, dv), BF16)], axis=1)
    st = state_ref[...]
    r = jnp.dot(p, v_aug, preferred_element_type=F32)
    eb = jnp.exp(bc)
    r = r + jnp.concatenate([eb, eb], axis=1) * jnp.dot(q, st.astype(BF16), preferred_element_type=F32)
    num = r[:, :dv]
    den = r[:, dv:]
    h = num / jnp.maximum(jnp.abs(den), 1.0)
    h = h * lax.rsqrt(jnp.mean(h * h, axis=-1, keepdims=True) + EPS)
    h = h * gh * jax.nn.sigmoid(og.astype(F32))

    w = jnp.exp(b_last - bc + ig)
    vw = jnp.concatenate([v.astype(F32) * w, w], axis=1).astype(BF16)
    upd = lax.dot_general(k, vw, (((0,), (0,)), ((), ())), preferred_element_type=F32)
    state_ref[...] = jnp.exp(b_last) * st + upd
    return h


def _mlstm_kernel(q_ref, k_ref, v_ref, og_ref, ig_ref, bc_ref, gh_ref, o_ref, state_ref):
    tl = q_ref.shape[0]
    dv = MLSTM_V_DIM
    w = V7X_LANES

    @pl.when(pl.program_id(1) == 0)
    def _():
        state_ref[...] = jnp.zeros_like(state_ref)

    row = lax.broadcasted_iota(jnp.int32, (tl, tl), 0)
    col = lax.broadcasted_iota(jnp.int32, (tl, tl), 1)
    causal = col <= row
    for head in range(MLSTM_HEADS):
        pair = slice((head // 2) * w, (head // 2 + 1) * w)
        mine = slice(head * dv, (head + 1) * dv)
        h = _mlstm_head(head, q_ref[:, pair], k_ref[:, pair], v_ref[:, mine], og_ref[:, mine],
                        ig_ref[:, head:head + 1], bc_ref[:, MLSTM_HEADS + head:MLSTM_HEADS + head + 1],
                        gh_ref[:, mine], causal, state_ref.at[head])
        o_ref[:, mine] = h.astype(o_ref.dtype)


def mlstm_core(proj, igs, bcum, g_h, batch, seq):
    t = proj.shape[0]
    tl = MLSTM_TILE
    nt = seq // tl
    heads = MLSTM_HEADS
    dv = MLSTM_V_DIM
    w = V7X_LANES
    qk = heads * MLSTM_QK_DIM
    hv = heads * dv

    def rows(b, i):
        return b * nt + i

    nbytes = 2 * (2 * tl * qk * 2 + 3 * tl * hv * 2 + 2 * tl * w * 4) + heads * w * 2 * dv * 4 + 16 * tl * tl * 4
    return pl.pallas_call(
        _mlstm_kernel,
        grid=(batch, nt),
        in_specs=[
            pl.BlockSpec((tl, qk), lambda b, i: (rows(b, i), 0)),
            pl.BlockSpec((tl, qk), lambda b, i: (rows(b, i), 1)),
            pl.BlockSpec((tl, hv), lambda b, i: (rows(b, i), 2 * qk // hv)),
            pl.BlockSpec((tl, hv), lambda b, i: (rows(b, i), 2 * qk // hv + 1)),
            pl.BlockSpec((tl, w), lambda b, i: (rows(b, i), 0)),
            pl.BlockSpec((tl, w), lambda b, i: (rows(b, i), 0)),
            pl.BlockSpec((1, hv), lambda b, i: (0, 0)),
        ],
        out_specs=pl.BlockSpec((tl, hv), lambda b, i: (rows(b, i), 0)),
        out_shape=jax.ShapeDtypeStruct((t, hv), BF16),
        scratch_shapes=[pltpu.VMEM((heads, w, 2 * dv), F32)],
        compiler_params=_params(("parallel", "arbitrary"), nbytes),
        name="mlstm_core",
    )(proj, proj, proj, proj, igs, bcum, g_h.reshape(1, hv))


def _key_proj_kernel(x_ref, g_ref, w_ref, gk_ref, o_ref, xn_ref):
    @pl.when(pl.program_id(1) == 0)
    def _():
        xn_ref[...] = _rms_scale(x_ref[...], g_ref[...]).astype(BF16)

    y = jnp.dot(xn_ref[...], w_ref[...], preferred_element_type=F32)
    d = SB_HEAD_DIM
    for c in range(y.shape[1] // V7X_LANES):
        yt = y[:, c * V7X_LANES:(c + 1) * V7X_LANES].T
        for half in range(V7X_LANES // d):
            blk = yt[half * d:(half + 1) * d, :]
            ms = jnp.mean(blk * blk, axis=0, keepdims=True)
            kn = blk * lax.rsqrt(ms + EPS) * gk_ref[...]
            o_ref[0, c, half * d:(half + 1) * d, :] = kn.astype(o_ref.dtype)


def key_proj(x, g, w, g_k, batch, seq):
    t, d = x.shape
    n = w.shape[1]
    tm = ROW_TILE
    tn = 512
    ns = seq // tm
    pairs = n // V7X_LANES
    nbytes = 2 * (tm * d * 4 + d * tn * 2 + tm * tn * 2) + tm * d * 2 + 2 * tm * tn * 4
    return pl.pallas_call(
        _key_proj_kernel,
        grid=(t // tm, n // tn),
        in_specs=[
            pl.BlockSpec((tm, d), lambda i, j: (i, 0)),
            pl.BlockSpec((1, d), lambda i, j: (0, 0)),
            pl.BlockSpec((d, tn), lambda i, j: (0, j)),
            pl.BlockSpec((SB_HEAD_DIM, 1), lambda i, j: (0, 0)),
        ],
        out_specs=pl.BlockSpec((1, tn // V7X_LANES, V7X_LANES, tm), lambda i, j: (i // ns, j, 0, i % ns)),
        out_shape=jax.ShapeDtypeStruct((batch, pairs, V7X_LANES, seq), BF16),
        scratch_shapes=[pltpu.VMEM((tm, d), BF16)],
        compiler_params=_params(("parallel", "arbitrary"), nbytes),
        name="key_proj",
    )(x, g.reshape(1, d), w, g_k.reshape(SB_HEAD_DIM, 1))


LOG2E = 1.4426950408889634


def _sb_scores(qh, kt, z_ref):
    z_ref[...] = jnp.dot(qh, kt, preferred_element_type=F32)


def _sb_sticks(z_ref, upper, run, first_valid_col, zs_ref, bt_ref):
    tq, tk = z_ref.shape
    z = z_ref[...]
    sp = jnp.maximum(z, 0.0) + jnp.log(1.0 + jnp.exp2(-jnp.abs(z))) * LOG2E
    zs = (z - sp) - run
    if first_valid_col is not None:
        valid = lax.broadcasted_iota(jnp.int32, (tq, tk), 1) < first_valid_col
        sp = jnp.where(valid, sp, 0.0)
        zs = jnp.where(valid, zs, -jnp.inf)
    zs_ref[...] = zs
    bt_ref[...] = jnp.dot(sp.astype(BF16), upper, preferred_element_type=F32)
    return run + jnp.sum(sp, axis=-1, keepdims=True)


def _sb_values(zs_ref, bt_ref, vb, acc_ref):
    a = jnp.exp2(zs_ref[...] - bt_ref[...])
    acc_ref[...] += jnp.dot(a.astype(BF16), vb, preferred_element_type=F32)


def _sb_kernel(q_ref, gq_ref, kt_ref, v_ref, up_ref, o_ref, z_ref, zs_ref, bt_ref, acc_ref):
    i = pl.program_id(2)
    tq = q_ref.shape[0]
    tk = up_ref.shape[0]
    assert tq == 2 * tk, "the schedule below visits the key blocks of a query tile in pairs"
    d = SB_HEAD_DIM
    x = q_ref[...]
    lane = lax.broadcasted_iota(jnp.int32, x.shape, 1)
    first = lane < d
    x2 = x * x
    ss_a = jnp.sum(jnp.where(first, x2, 0.0), axis=-1, keepdims=True)
    ss_b = jnp.sum(jnp.where(first, 0.0, x2), axis=-1, keepdims=True)
    inv = jnp.where(first, lax.rsqrt(ss_a / d + EPS), lax.rsqrt(ss_b / d + EPS))
    qn = x * inv * gq_ref[...] * (LOG2E * d ** -0.5)
    heads = (jnp.where(first, qn, 0.0).astype(BF16), jnp.where(first, 0.0, qn).astype(BF16))
    upper = up_ref[...]
    row = lax.broadcasted_iota(jnp.int32, (tq, 1), 0)
    n_blocks = 2 * i + 2

    def start(n):
        kb = jnp.maximum(n_blocks - 1 - n, 0)
        return pl.multiple_of(kb * tk, tk)

    everything = pl.ds(0, tq)
    second_half = pl.ds(tk, tk)

    def scores(n, slot, rows=everything):
        kt = kt_ref[0, 0, :, pl.ds(start(n), tk)]
        for h in range(2):
            _sb_scores(heads[h][rows.start:rows.start + rows.size], kt, z_ref.at[slot, h, rows])

    def sticks(slot, runs, first_valid_col, rows=everything):
        return tuple(_sb_sticks(z_ref.at[slot, h, rows], upper, runs[h], first_valid_col, zs_ref.at[slot, h, rows],
                                bt_ref.at[slot, h, rows]) for h in range(2))

    def values(n, slot, rows=everything):
        vb = v_ref[pl.ds(start(n), tk), :]
        for h in range(2):
            _sb_values(zs_ref.at[slot, h, rows], bt_ref.at[slot, h, rows], vb, acc_ref.at[h, rows])

    acc_ref[...] = jnp.zeros_like(acc_ref)
    half_zero = jnp.zeros((tk, 1), F32)
    scores(0, 0, second_half)
    scores(1, 1)
    runs = sticks(0, (half_zero, half_zero), row[:tk], second_half)
    runs = tuple(jnp.concatenate([half_zero, r], axis=0) for r in runs)
    scores(2, 0)
    runs = sticks(1, runs, row)
    values(0, 0, second_half)

    def body(m, runs):
        n = 2 + 2 * m
        scores(n + 1, 1)
        runs = sticks(0, runs, None)
        values(n - 1, 1)
        scores(n + 2, 0)
        runs = sticks(1, runs, None)
        values(n, 0)
        return runs

    lax.fori_loop(0, i, body, runs)
    values(n_blocks - 1, 1)
    o_ref[...] = jnp.where(first, acc_ref[0], acc_ref[1]).astype(o_ref.dtype)


def stick_breaking(qraw, g_q, kt, v, batch, seq):
    t, width = qraw.shape
    tq = SB_Q_TILE
    tk = SB_K_TILE
    nq = seq // tq
    w = V7X_LANES
    pairs = width // w
    r = lax.broadcasted_iota(jnp.int32, (tk, tk), 0)
    c = lax.broadcasted_iota(jnp.int32, (tk, tk), 1)
    upper = (r > c).astype(BF16)
    gq2 = jnp.tile(g_q.reshape(1, SB_HEAD_DIM), (1, w // SB_HEAD_DIM))
    nbytes = 2 * (tq * w * 4 + 2 * seq * w * 2 + tk * tk * 2 + tq * w * 2) + 32 * tq * tk * 4
    return pl.pallas_call(
        _sb_kernel,
        grid=(batch, pairs, nq),
        in_specs=[
            pl.BlockSpec((tq, w), lambda b, p, i: (b * nq + i, p)),
            pl.BlockSpec((1, w), lambda b, p, i: (0, 0)),
            pl.BlockSpec((1, 1, w, seq), lambda b, p, i: (b, p, 0, 0)),
            pl.BlockSpec((seq, w), lambda b, p, i: (b, p)),
            pl.BlockSpec((tk, tk), lambda b, p, i: (0, 0)),
        ],
        out_specs=pl.BlockSpec((tq, w), lambda b, p, i: (b * nq + i, p)),
        out_shape=jax.ShapeDtypeStruct((t, width), BF16),
        scratch_shapes=[
            pltpu.VMEM((2, 2, tq, tk), F32),
            pltpu.VMEM((2, 2, tq, tk), F32),
            pltpu.VMEM((2, 2, tq, tk), F32),
            pltpu.VMEM((2, tq, w), F32),
        ],
        compiler_params=_params(("parallel", "parallel", "arbitrary"), nbytes),
        name="stick_breaking",
    )(qraw, gq2, kt, v, upper)


def _ffn_kernel(x_ref, g_ref, wg_ref, wu_ref, wd_ref, o_ref, xn_ref, acc_ref):
    f = pl.program_id(1)

    @pl.when(f == 0)
    def _():
        xn_ref[...] = _rms_scale(x_ref[...], g_ref[...]).astype(BF16)
        acc_ref[...] = jnp.zeros_like(acc_ref)

    xn = xn_ref[...]
    hg = jnp.dot(xn, wg_ref[...], preferred_element_type=F32)
    hu = jnp.dot(xn, wu_ref[...], preferred_element_type=F32)
    act = (jax.nn.silu(hg) * hu).astype(BF16)
    acc_ref[...] += jnp.dot(act, wd_ref[...], preferred_element_type=F32)

    @pl.when(f == pl.num_programs(1) - 1)
    def _():
        o_ref[...] = x_ref[...] + acc_ref[...]


def ffn_dense(x, g, wg, wu, wd):
    t, d = x.shape
    f = wg.shape[1]
    tm = FFN_ROW_TILE
    tf = FFN_COL_TILE
    nbytes = 2 * (2 * tm * d * 4 + 3 * d * tf * 2) + tm * d * 6 + 4 * tm * tf * 4
    return pl.pallas_call(
        _ffn_kernel,
        grid=(t // tm, f // tf),
        in_specs=[
            pl.BlockSpec((tm, d), lambda i, j: (i, 0)),
            pl.BlockSpec((1, d), lambda i, j: (0, 0)),
            pl.BlockSpec((d, tf), lambda i, j: (0, j)),
            pl.BlockSpec((d, tf), lambda i, j: (0, j)),
            pl.BlockSpec((tf, d), lambda i, j: (j, 0)),
        ],
        out_specs=pl.BlockSpec((tm, d), lambda i, j: (i, 0)),
        out_shape=jax.ShapeDtypeStruct((t, d), F32),
        scratch_shapes=[pltpu.VMEM((tm, d), BF16), pltpu.VMEM((tm, d), F32)],
        compiler_params=_params(("parallel", "arbitrary"), nbytes),
        name="ffn_dense",
    )(x, g.reshape(1, d), wg, wu, wd)


def _router_kernel(x_ref, g_ref, wr_ref, tri_ref, xn_ref, mi_ref, mf_ref, cnt_ref, run_ref):
    @pl.when(pl.program_id(0) == 0)
    def _():
        run_ref[...] = jnp.zeros_like(run_ref)

    xn = _rms_scale(x_ref[...], g_ref[...])
    xn_ref[...] = xn
    logits = jnp.dot(xn.astype(BF16), wr_ref[...], preferred_element_type=F32)
    lane = lax.broadcasted_iota(jnp.int32, logits.shape, 1)
    neg = jnp.asarray(-jnp.inf, F32)
    logits = jnp.where(lane < N_EXPERTS, logits, neg)
    m1 = jnp.max(logits, axis=-1, keepdims=True)
    i1 = jnp.min(jnp.where(logits == m1, lane, V7X_LANES), axis=-1, keepdims=True)
    rest = jnp.where(lane == i1, neg, logits)
    m2 = jnp.max(rest, axis=-1, keepdims=True)
    i2 = jnp.min(jnp.where(rest == m2, lane, V7X_LANES), axis=-1, keepdims=True)
    e2 = jnp.exp(m2 - m1)
    denom = 1.0 + e2
    mf_ref[...] = jnp.where(lane == 0, 1.0 / denom, jnp.where(lane == 1, e2 / denom, 0.0))

    chosen = jnp.logical_or(lane == i1, lane == i2)
    onehot = jnp.where(chosen, 1.0, 0.0)
    rank = jnp.dot(tri_ref[...], onehot.astype(BF16), preferred_element_type=F32) + run_ref[...]
    r1 = jnp.sum(jnp.where(lane == i1, rank, 0.0), axis=-1, keepdims=True).astype(jnp.int32)
    r2 = jnp.sum(jnp.where(lane == i2, rank, 0.0), axis=-1, keepdims=True).astype(jnp.int32)
    mi_ref[...] = jnp.where(lane == 0, i1, jnp.where(lane == 1, i2, jnp.where(lane == 2, r1, jnp.where(lane == 3, r2, 0))))
    run_ref[...] += jnp.sum(onehot, axis=0, keepdims=True)
    cnt_ref[...] = run_ref[...]


def router(x, g, w_router_padded):
    t, d = x.shape
    tm = ROW_TILE
    w = V7X_LANES
    r = lax.broadcasted_iota(jnp.int32, (tm, tm), 0)
    c = lax.broadcasted_iota(jnp.int32, (tm, tm), 1)
    tri = (c < r).astype(BF16)
    nbytes = 2 * (2 * tm * d * 4 + d * w * 2 + tm * tm * 2 + 2 * tm * w * 4) + 2 * tm * d * 4
    row_spec = pl.BlockSpec((tm, w), lambda i: (i, 0))
    return pl.pallas_call(
        _router_kernel,
        grid=(t // tm,),
        in_specs=[
            pl.BlockSpec((tm, d), lambda i: (i, 0)),
            pl.BlockSpec((1, d), lambda i: (0, 0)),
            pl.BlockSpec((d, w), lambda i: (0, 0)),
            pl.BlockSpec((tm, tm), lambda i: (0, 0)),
        ],
        out_specs=[pl.BlockSpec((tm, d), lambda i: (i, 0)), row_spec, row_spec, pl.BlockSpec((1, w), lambda i: (0, 0))],
        out_shape=[
            jax.ShapeDtypeStruct((t, d), F32),
            jax.ShapeDtypeStruct((t, w), jnp.int32),
            jax.ShapeDtypeStruct((t, w), F32),
            jax.ShapeDtypeStruct((1, w), F32),
        ],
        scratch_shapes=[pltpu.VMEM((1, w), F32)],
        compiler_params=_params(("arbitrary",), nbytes),
        name="router",
    )(x, g.reshape(1, d), w_router_padded, tri)


def _row_copy(src, src_row, dst, dst_row, sem):
    return pltpu.make_async_copy(src.at[pl.ds(src_row, 1)], dst.at[pl.ds(dst_row, 1)], sem)


def _dispatch_kernel(slots_ref, last_ref, xn_hbm, xs_hbm, zero_ref, buf_ref, load_sems, row_sems, zero_sem):
    step = pl.program_id(0)
    last_step = pl.num_programs(0) - 1
    td = buf_ref.shape[1]
    tg = zero_ref.shape[0]

    def load(i, parity):
        rows = pl.ds(pl.multiple_of(i * td, td), td)
        return pltpu.make_async_copy(xn_hbm.at[rows], buf_ref.at[parity], load_sems.at[parity])

    def wait_rows(parity):
        for k in range(2):
            pltpu.make_async_copy(buf_ref.at[parity], xs_hbm.at[pl.ds(0, td)], row_sems.at[parity]).wait()

    @pl.when(step == 0)
    def _():
        load(0, 0).start()
        zero_ref[...] = jnp.zeros_like(zero_ref)

        def fill(e):
            return pltpu.make_async_copy(zero_ref, xs_hbm.at[pl.ds(pl.multiple_of(last_ref[e], tg), tg)], zero_sem)

        for e in range(last_ref.shape[0]):
            @pl.when(last_ref[e] >= 0)
            def _(e=e):
                fill(e).start()

        for e in range(last_ref.shape[0]):
            @pl.when(last_ref[e] >= 0)
            def _(e=e):
                fill(e).wait()

    for parity in range(2):
        @pl.when(step % 2 == parity)
        def _(parity=parity):
            @pl.when(step >= 1)
            def _():
                wait_rows(1 - parity)

            @pl.when(step < last_step)
            def _():
                load(step + 1, 1 - parity).start()

            load(step, parity).wait()

            def body(r, carry):
                for k in range(2):
                    _row_copy(buf_ref.at[parity], r, xs_hbm, slots_ref[2 * r + k], row_sems.at[parity]).start()
                return carry

            lax.fori_loop(0, td, body, 0, unroll=8)

            @pl.when(step == last_step)
            def _():
                wait_rows(parity)


def dispatch(slots, last_tile_row, xn, n_slots):
    t, d = xn.shape
    td = MOE_ROUTE_TILE
    tg = MOE_GROUP_TILE
    return pl.pallas_call(
        _dispatch_kernel,
        grid=(t // td,),
        in_specs=[
            pl.BlockSpec((2 * td,), lambda i: (i,), memory_space=pltpu.SMEM),
            pl.BlockSpec(memory_space=pltpu.SMEM),
            pl.BlockSpec(memory_space=pl.ANY),
        ],
        out_specs=pl.BlockSpec(memory_space=pl.ANY),
        out_shape=jax.ShapeDtypeStruct((n_slots, d), xn.dtype),
        scratch_shapes=[
            pltpu.VMEM((tg, d), xn.dtype),
            pltpu.VMEM((2, td, d), xn.dtype),
            pltpu.SemaphoreType.DMA((2,)),
            pltpu.SemaphoreType.DMA((2,)),
            pltpu.SemaphoreType.DMA,
        ],
        compiler_params=_params(("arbitrary",), 2 * td * d * 4 + tg * d * 4),
        name="moe_dispatch",
    )(slots, last_tile_row, xn)


def _experts_kernel(te_ref, na_ref, xs_ref, wg_ref, wu_ref, wd_ref, ys_ref, xb_ref, acc_ref):
    del te_ref
    f = pl.program_id(1)
    active = pl.program_id(0) < na_ref[0]

    @pl.when(jnp.logical_and(jnp.logical_not(active), f == 0))
    def _():
        ys_ref[...] = jnp.zeros_like(ys_ref)

    @pl.when(active)
    def _():
        @pl.when(f == 0)
        def _():
            xb_ref[...] = xs_ref[...].astype(BF16)
            acc_ref[...] = jnp.zeros_like(acc_ref)

        xb = xb_ref[...]
        hg = jnp.dot(xb, wg_ref[0], preferred_element_type=F32)
        hu = jnp.dot(xb, wu_ref[0], preferred_element_type=F32)
        act = (jax.nn.silu(hg) * hu).astype(BF16)
        acc_ref[...] += jnp.dot(act, wd_ref[0], preferred_element_type=F32)

        @pl.when(f == pl.num_programs(1) - 1)
        def _():
            ys_ref[...] = acc_ref[...]


def experts(tile_expert, n_active, xs, wg, wu, wd):
    n_slots, d = xs.shape
    fe = wg.shape[2]
    tg = MOE_GROUP_TILE
    tf = MOE_COL_TILE
    nf = fe // tf
    n_tiles = n_slots // tg

    def tile(n, na):
        return jnp.minimum(n, na[0] - 1)

    def col(n, f, na):
        return jnp.where(n < na[0], f, nf - 1)

    nbytes = 2 * (2 * tg * d * 4 + 3 * d * tf * 2) + tg * d * 6 + 4 * tg * tf * 4
    return pl.pallas_call(
        _experts_kernel,
        grid_spec=pltpu.PrefetchScalarGridSpec(
            num_scalar_prefetch=2,
            grid=(n_tiles, nf),
            in_specs=[
                pl.BlockSpec((tg, d), lambda n, f, te, na: (tile(n, na), 0)),
                pl.BlockSpec((1, d, tf), lambda n, f, te, na: (te[tile(n, na)], 0, col(n, f, na))),
                pl.BlockSpec((1, d, tf), lambda n, f, te, na: (te[tile(n, na)], 0, col(n, f, na))),
                pl.BlockSpec((1, tf, d), lambda n, f, te, na: (te[tile(n, na)], col(n, f, na), 0)),
            ],
            out_specs=pl.BlockSpec((tg, d), lambda n, f, te, na: (n, 0)),
            scratch_shapes=[pltpu.VMEM((tg, d), BF16), pltpu.VMEM((tg, d), F32)],
        ),
        out_shape=jax.ShapeDtypeStruct((n_slots, d), F32),
        compiler_params=_params(("arbitrary", "arbitrary"), nbytes),
        name="moe_experts",
    )(tile_expert, n_active, xs, wg, wu, wd)


def _combine_kernel(slots_ref, next_slots_ref, x_ref, mf_ref, ys_hbm, o_ref, buf_ref, sems):
    step = pl.program_id(0)
    tc = x_ref.shape[0]

    def gather(s_ref, parity):
        def body(r, carry):
            for k in range(2):
                _row_copy(ys_hbm, s_ref[2 * r + k], buf_ref.at[parity, k], r, sems.at[parity]).start()
            return carry

        lax.fori_loop(0, tc, body, 0, unroll=8)

    @pl.when(step == 0)
    def _():
        gather(slots_ref, 0)

    lane = lax.broadcasted_iota(jnp.int32, mf_ref.shape, 1)
    mf = mf_ref[...]
    w1 = jnp.sum(jnp.where(lane == 0, mf, 0.0), axis=-1, keepdims=True)
    w2 = jnp.sum(jnp.where(lane == 1, mf, 0.0), axis=-1, keepdims=True)
    for parity in range(2):
        @pl.when(step % 2 == parity)
        def _(parity=parity):
            @pl.when(step + 1 < pl.num_programs(0))
            def _():
                gather(next_slots_ref, 1 - parity)

            for k in range(2):
                pltpu.make_async_copy(ys_hbm.at[pl.ds(0, tc)], buf_ref.at[parity, k], sems.at[parity]).wait()
            o_ref[...] = x_ref[...] + (w1 * buf_ref[parity, 0] + w2 * buf_ref[parity, 1])


def combine(slots, x, meta_f, ys):
    t, d = x.shape
    tc = MOE_ROUTE_TILE
    w = meta_f.shape[1]
    steps = t // tc
    nbytes = 2 * (2 * tc * d * 4 + tc * w * 4) + 4 * tc * d * 4
    return pl.pallas_call(
        _combine_kernel,
        grid=(steps,),
        in_specs=[
            pl.BlockSpec((2 * tc,), lambda i: (i,), memory_space=pltpu.SMEM),
            pl.BlockSpec((2 * tc,), lambda i: (jnp.minimum(i + 1, steps - 1),), memory_space=pltpu.SMEM),
            pl.BlockSpec((tc, d), lambda i: (i, 0)),
            pl.BlockSpec((tc, w), lambda i: (i, 0)),
            pl.BlockSpec(memory_space=pl.ANY),
        ],
        out_specs=pl.BlockSpec((tc, d), lambda i: (i, 0)),
        out_shape=jax.ShapeDtypeStruct((t, d), F32),
        scratch_shapes=[pltpu.VMEM((2, 2, tc, d), F32), pltpu.SemaphoreType.DMA((2,))],
        compiler_params=_params(("arbitrary",), nbytes),
        name="moe_combine",
    )(slots, slots, x, meta_f, ys)


def moe_routed(x, g, w_router, wg, wu, wd):
    t, d = x.shape
    w = V7X_LANES
    tg = MOE_GROUP_TILE
    n_tiles = 2 * t // tg + N_EXPERTS
    w_r = jnp.pad(w_router, ((0, 0), (0, w - N_EXPERTS))).astype(BF16)
    xn, meta_i, meta_f, counts = router(x, g, w_r)

    counts = counts[0, :N_EXPERTS].astype(jnp.int32)
    tiles_per = (counts + tg - 1) // tg
    tile_end = jnp.cumsum(tiles_per)
    group_start = (tile_end - tiles_per) * tg
    n_active = tile_end[-1:].astype(jnp.int32)
    tile_ids = jnp.arange(n_tiles, dtype=jnp.int32)
    tile_expert = jnp.minimum(jnp.sum(tile_ids[:, None] >= tile_end[None, :], axis=1), N_EXPERTS - 1).astype(jnp.int32)
    slots = (jnp.take(group_start, meta_i[:, 0:2]) + meta_i[:, 2:4]).reshape(-1).astype(jnp.int32)

    trailing = n_tiles - 1 - jnp.arange(N_EXPERTS, dtype=jnp.int32)
    fill_rows = jnp.concatenate([jnp.where(tiles_per > 0, (tile_end - 1) * tg, -1),
                                 jnp.where(trailing >= n_active[0], trailing * tg, -1)]).astype(jnp.int32)
    xs = dispatch(slots, fill_rows, xn, n_tiles * tg)
    ys = experts(tile_expert, n_active, xs, wg, wu, wd)
    return combine(slots, x, meta_f, ys)


def kernel(x, mix_norm, ffn_norm, mlstm_w_in, mlstm_b_igate, mlstm_b_fgate, mlstm_g_h, mlstm_w_out, kv_norm, w_kv, g_k, sb_w_q, sb_g_q, sb_w_o, ffn_w_gate, ffn_w_up, ffn_w_down, moe_w_router, moe_w_gate, moe_w_up, moe_w_down):
    batch, seq, d = x.shape
    xt = x.reshape(batch * seq, d)
    w = V7X_LANES
    n_main = 2 * MLSTM_HEADS * MLSTM_QK_DIM + 2 * MLSTM_HEADS * MLSTM_V_DIM

    w_in = mlstm_w_in[0]
    w_main = w_in[:, :n_main].astype(BF16)
    w_gate = jnp.pad(w_in[:, n_main:], ((0, 0), (0, w - 2 * MLSTM_HEADS))).astype(BF16)
    bias = jnp.pad(jnp.concatenate([mlstm_b_igate[0], mlstm_b_fgate[0]]), (0, w - 2 * MLSTM_HEADS)).reshape(1, w)
    proj = norm_matmul(xt, mix_norm[0], w_main, BF16, tn=PROJ_COL_TILE)
    graw = norm_matmul(xt, mix_norm[0], w_gate, F32, tn=w)
    igs, bcum = gate_prep(graw, bias)
    hmix = mlstm_core(proj, igs, bcum, mlstm_g_h[0], batch, seq)
    xt = matmul_residual(hmix, mlstm_w_out[0].astype(BF16), xt)

    xt = ffn_dense(xt, ffn_norm[0], ffn_w_gate[0].astype(BF16), ffn_w_up[0].astype(BF16), ffn_w_down[0].astype(BF16))

    sb_width = SB_HEADS * SB_HEAD_DIM
    kt = key_proj(xt, kv_norm, w_kv[:, :sb_width].astype(BF16), g_k, batch, seq)
    v = norm_matmul(xt, kv_norm, w_kv[:, sb_width:].astype(BF16), BF16, tn=PROJ_COL_TILE)
    qraw = norm_matmul(xt, mix_norm[1], sb_w_q[0].astype(BF16), F32, tn=PROJ_COL_TILE)
    o = stick_breaking(qraw, sb_g_q[0], kt, v, batch, seq)
    xt = matmul_residual(o, sb_w_o[0].astype(BF16), xt)

    xt = moe_routed(xt, ffn_norm[1], moe_w_router[0], moe_w_gate[0].astype(BF16), moe_w_up[0].astype(BF16), moe_w_down[0].astype(BF16))
    return xt.reshape(batch, seq, d)
```

```python
import functools

import jax
import jax.numpy as jnp
from jax import lax
from jax.experimental import pallas as pl
from jax.experimental.pallas import tpu as pltpu

F32 = jnp.float32
BF16 = jnp.bfloat16

EPS = 1e-6
GATE_SOFTCAP = 15.0
MLSTM_HEADS = 8
MLSTM_QK_DIM = 64
MLSTM_V_DIM = 128
SB_HEADS = 16
SB_HEAD_DIM = 64
N_EXPERTS = 8

V7X_LANES = 128
V7X_VMEM_BYTES = 64 * 1024 * 1024

ROW_TILE = 512
PROJ_ROW_TILE = 1024
PROJ_COL_TILE = 1024
FFN_ROW_TILE = 1024
FFN_COL_TILE = 256
MOE_COL_TILE = 512
MLSTM_TILE = 256
MOE_GROUP_TILE = 1024
MOE_ROUTE_TILE = 256
SB_Q_TILE = 512
SB_K_TILE = 256


def _vmem_limit(nbytes):
    return int(min(max(2 * nbytes, 32 * 1024 * 1024), V7X_VMEM_BYTES - 8 * 1024 * 1024))


def _params(semantics, nbytes):
    return pltpu.CompilerParams(dimension_semantics=semantics, vmem_limit_bytes=_vmem_limit(nbytes))


def _rms_scale(x, g):
    ms = jnp.mean(x * x, axis=-1, keepdims=True)
    return x * lax.rsqrt(ms + EPS) * g


def _norm_matmul_kernel(x_ref, g_ref, w_ref, o_ref, xn_ref):
    @pl.when(pl.program_id(1) == 0)
    def _():
        xn_ref[...] = _rms_scale(x_ref[...], g_ref[...]).astype(BF16)

    o_ref[...] = jnp.dot(xn_ref[...], w_ref[...], preferred_element_type=F32).astype(o_ref.dtype)


def norm_matmul(x, g, w, out_dtype, tn):
    t, d = x.shape
    n = w.shape[1]
    tm = PROJ_ROW_TILE
    nbytes = 2 * (tm * d * 4 + d * tn * 2 + tm * tn * 4) + tm * d * 2
    return pl.pallas_call(
        _norm_matmul_kernel,
        grid=(t // tm, n // tn),
        in_specs=[
            pl.BlockSpec((tm, d), lambda i, j: (i, 0)),
            pl.BlockSpec((1, d), lambda i, j: (0, 0)),
            pl.BlockSpec((d, tn), lambda i, j: (0, j)),
        ],
        out_specs=pl.BlockSpec((tm, tn), lambda i, j: (i, j)),
        out_shape=jax.ShapeDtypeStruct((t, n), out_dtype),
        scratch_shapes=[pltpu.VMEM((tm, d), BF16)],
        compiler_params=_params(("parallel", "arbitrary"), nbytes),
        name="norm_matmul",
    )(x, g.reshape(1, d), w)


def _matmul_residual_kernel(a_ref, w_ref, r_ref, o_ref):
    o_ref[...] = r_ref[...] + jnp.dot(a_ref[...], w_ref[...], preferred_element_type=F32)


def matmul_residual(a, w, res):
    t, k = a.shape
    n = w.shape[1]
    tm = PROJ_ROW_TILE
    nbytes = 2 * (tm * k * 2 + k * n * 2 + 2 * tm * n * 4)
    return pl.pallas_call(
        _matmul_residual_kernel,
        grid=(t // tm,),
        in_specs=[
            pl.BlockSpec((tm, k), lambda i: (i, 0)),
            pl.BlockSpec((k, n), lambda i: (0, 0)),
            pl.BlockSpec((tm, n), lambda i: (i, 0)),
        ],
        out_specs=pl.BlockSpec((tm, n), lambda i: (i, 0)),
        out_shape=jax.ShapeDtypeStruct((t, n), F32),
        compiler_params=_params(("parallel",), nbytes),
        name="matmul_residual",
    )(a, w, res)


def _split3(x):
    hi = x.astype(BF16)
    r1 = x - hi.astype(F32)
    mid = r1.astype(BF16)
    lo = (r1 - mid.astype(F32)).astype(BF16)
    return hi, mid, lo


def _gate_prep_kernel(x_ref, g_ref, wg_ref, b_ref, tri_ref, ig_ref, bc_ref):
    xn = _rms_scale(x_ref[...], g_ref[...]).astype(BF16)
    z = jnp.dot(xn, wg_ref[...], preferred_element_type=F32) + b_ref[...]
    z = GATE_SOFTCAP * jnp.tanh(z / GATE_SOFTCAP)
    ig_ref[...] = z
    lf = jax.nn.log_sigmoid(z)
    hi, mid, lo = _split3(lf)
    parts = jnp.dot(tri_ref[...], jnp.concatenate([hi, mid, lo], axis=1), preferred_element_type=F32)
    w = V7X_LANES
    bc_ref[...] = (parts[:, :w] + parts[:, w:2 * w]) + parts[:, 2 * w:]


def gate_prep(x, g, w_gate, bias):
    t, d = x.shape
    w = w_gate.shape[1]
    tl = MLSTM_TILE
    r = lax.broadcasted_iota(jnp.int32, (tl, tl), 0)
    c = lax.broadcasted_iota(jnp.int32, (tl, tl), 1)
    tri = (c <= r).astype(BF16)
    spec = pl.BlockSpec((tl, w), lambda i: (i, 0))
    return pl.pallas_call(
        _gate_prep_kernel,
        grid=(t // tl,),
        in_specs=[
            pl.BlockSpec((tl, d), lambda i: (i, 0)),
            pl.BlockSpec((1, d), lambda i: (0, 0)),
            pl.BlockSpec((d, w), lambda i: (0, 0)),
            pl.BlockSpec((1, w), lambda i: (0, 0)),
            pl.BlockSpec((tl, tl), lambda i: (0, 0)),
        ],
        out_specs=[spec, spec],
        out_shape=[jax.ShapeDtypeStruct((t, w), F32)] * 2,
        compiler_params=_params(("parallel",), 2 * tl * d * 4 + d * w * 2 + 8 * tl * w * 4),
        name="gate_prep",
    )(x, g.reshape(1, d), w_gate, bias, tri)


def _mlstm_head(head, q_pair, k_pair, v, og, ig, bc, gh, causal, state_ref):
    tl, dv = v.shape
    lane = lax.broadcasted_iota(jnp.int32, (tl, V7X_LANES), 1)
    mine = (lane // MLSTM_QK_DIM) == (head % 2)
    zero = jnp.zeros((), BF16)
    q = jnp.where(mine, q_pair, zero) * jnp.asarray(MLSTM_QK_DIM ** -0.5, BF16)
    k = jnp.where(mine, k_pair, zero)
    assert dv == V7X_LANES
    b_last = bc[tl - 1:tl, :]
    bc = jnp.broadcast_to(bc, (tl, dv))
    ig = jnp.broadcast_to(ig, (tl, dv))

    src = (bc - ig).T[0:1, :]
    decay = jnp.exp(jnp.where(causal, jnp.concatenate([bc] * (tl // dv), axis=1) - src, -jnp.inf))

    s = lax.dot_general(q, k, (((1,), (1,)), ((), ())), preferred_element_type=F32)
    p = (s * decay).astype(BF16)
    v_aug = jnp.concatenate([v, jnp.ones((tl, dv), BF16)], axis=1)
    st = state_ref[...]
    r = jnp.dot(p, v_aug, preferred_element_type=F32)
    eb = jnp.exp(bc)
    r = r + jnp.concatenate([eb, eb], axis=1) * jnp.dot(q, st.astype(BF16), preferred_element_type=F32)
    num = r[:, :dv]
    den = r[:, dv:]
    h = num / jnp.maximum(jnp.abs(den), 1.0)
    h = h * lax.rsqrt(jnp.mean(h * h, axis=-1, keepdims=True) + EPS)
    h = h * gh * jax.nn.sigmoid(og.astype(F32))

    w = jnp.exp(b_last - bc + ig)
    vw = jnp.concatenate([v.astype(F32) * w, w], axis=1).astype(BF16)
    upd = lax.dot_general(k, vw, (((0,), (0,)), ((), ())), preferred_element_type=F32)
    state_ref[...] = jnp.exp(b_last) * st + upd
    return h


def _mlstm_kernel(q_ref, k_ref, v_ref, og_ref, ig_ref, bc_ref, gh_ref, o_ref, state_ref):
    tl = q_ref.shape[0]
    dv = MLSTM_V_DIM
    w = V7X_LANES

    @pl.when(pl.program_id(1) == 0)
    def _():
        state_ref[...] = jnp.zeros_like(state_ref)

    row = lax.broadcasted_iota(jnp.int32, (tl, tl), 0)
    col = lax.broadcasted_iota(jnp.int32, (tl, tl), 1)
    causal = col <= row
    for head in range(MLSTM_HEADS):
        pair = slice((head // 2) * w, (head // 2 + 1) * w)
        mine = slice(head * dv, (head + 1) * dv)
        h = _mlstm_head(head, q_ref[:, pair], k_ref[:, pair], v_ref[:, mine], og_ref[:, mine],
                        ig_ref[:, head:head + 1], bc_ref[:, MLSTM_HEADS + head:MLSTM_HEADS + head + 1],
                        gh_ref[:, mine], causal, state_ref.at[head])
        o_ref[:, mine] = h.astype(o_ref.dtype)


def mlstm_core(proj, igs, bcum, g_h, batch, seq):
    t = proj.shape[0]
    tl = MLSTM_TILE
    nt = seq // tl
    heads = MLSTM_HEADS
    dv = MLSTM_V_DIM
    w = V7X_LANES
    qk = heads * MLSTM_QK_DIM
    hv = heads * dv

    def rows(b, i):
        return b * nt + i

    nbytes = 2 * (2 * tl * qk * 2 + 3 * tl * hv * 2 + 2 * tl * w * 4) + heads * w * 2 * dv * 4 + 16 * tl * tl * 4
    return pl.pallas_call(
        _mlstm_kernel,
        grid=(batch, nt),
        in_specs=[
            pl.BlockSpec((tl, qk), lambda b, i: (rows(b, i), 0)),
            pl.BlockSpec((tl, qk), lambda b, i: (rows(b, i), 1)),
            pl.BlockSpec((tl, hv), lambda b, i: (rows(b, i), 2 * qk // hv)),
            pl.BlockSpec((tl, hv), lambda b, i: (rows(b, i), 2 * qk // hv + 1)),
            pl.BlockSpec((tl, w), lambda b, i: (rows(b, i), 0)),
            pl.BlockSpec((tl, w), lambda b, i: (rows(b, i), 0)),
            pl.BlockSpec((1, hv), lambda b, i: (0, 0)),
        ],
        out_specs=pl.BlockSpec((tl, hv), lambda b, i: (rows(b, i), 0)),
        out_shape=jax.ShapeDtypeStruct((t, hv), BF16),
        scratch_shapes=[pltpu.VMEM((heads, w, 2 * dv), F32)],
        compiler_params=_params(("parallel", "arbitrary"), nbytes),
        name="mlstm_core",
    )(proj, proj, proj, proj, igs, bcum, g_h.reshape(1, hv))


def _key_proj_kernel(x_ref, g_ref, w_ref, gk_ref, o_ref, xn_ref):
    @pl.when(pl.program_id(1) == 0)
    def _():
        xn_ref[...] = _rms_scale(x_ref[...], g_ref[...]).astype(BF16)

    y = jnp.dot(xn_ref[...], w_ref[...], preferred_element_type=F32)
    d = SB_HEAD_DIM
    for c in range(y.shape[1] // V7X_LANES):
        yt = y[:, c * V7X_LANES:(c + 1) * V7X_LANES].T
        for half in range(V7X_LANES // d):
            blk = yt[half * d:(half + 1) * d, :]
            ms = jnp.mean(blk * blk, axis=0, keepdims=True)
            kn = blk * lax.rsqrt(ms + EPS) * gk_ref[...]
            o_ref[0, c, half * d:(half + 1) * d, :] = kn.astype(o_ref.dtype)


def key_proj(x, g, w, g_k, batch, seq):
    t, d = x.shape
    n = w.shape[1]
    tm = ROW_TILE
    tn = 512
    ns = seq // tm
    pairs = n // V7X_LANES
    nbytes = 2 * (tm * d * 4 + d * tn * 2 + tm * tn * 2) + tm * d * 2 + 2 * tm * tn * 4
    return pl.pallas_call(
        _key_proj_kernel,
        grid=(t // tm, n // tn),
        in_specs=[
            pl.BlockSpec((tm, d), lambda i, j: (i, 0)),
            pl.BlockSpec((1, d), lambda i, j: (0, 0)),
            pl.BlockSpec((d, tn), lambda i, j: (0, j)),
            pl.BlockSpec((SB_HEAD_DIM, 1), lambda i, j: (0, 0)),
        ],
        out_specs=pl.BlockSpec((1, tn // V7X_LANES, V7X_LANES, tm), lambda i, j: (i // ns, j, 0, i % ns)),
        out_shape=jax.ShapeDtypeStruct((batch, pairs, V7X_LANES, seq), BF16),
        scratch_shapes=[pltpu.VMEM((tm, d), BF16)],
        compiler_params=_params(("parallel", "arbitrary"), nbytes),
        name="key_proj",
    )(x, g.reshape(1, d), w, g_k.reshape(SB_HEAD_DIM, 1))


LOG2E = 1.4426950408889634


def _sb_scores(qh, kt, z_ref):
    z_ref[...] = jnp.dot(qh, kt, preferred_element_type=F32)


def _sb_sticks(z_ref, upper, run, first_valid_col, zs_ref, bt_ref):
    tq, tk = z_ref.shape
    z = z_ref[...]
    sp = jnp.maximum(z, 0.0) + jnp.log(1.0 + jnp.exp2(-jnp.abs(z))) * LOG2E
    zs = (z - sp) - run
    if first_valid_col is not None:
        valid = lax.broadcasted_iota(jnp.int32, (tq, tk), 1) < first_valid_col
        sp = jnp.where(valid, sp, 0.0)
        zs = jnp.where(valid, zs, -jnp.inf)
    zs_ref[...] = zs
    bt_ref[...] = jnp.dot(sp.astype(BF16), upper, preferred_element_type=F32)
    return run + jnp.sum(sp, axis=-1, keepdims=True)


def _sb_values(zs_ref, bt_ref, vb, acc_ref):
    a = jnp.exp2(zs_ref[...] - bt_ref[...])
    acc_ref[...] += jnp.dot(a.astype(BF16), vb, preferred_element_type=F32)


def _sb_kernel(q_ref, gq_ref, kt_ref, v_ref, up_ref, o_ref, z_ref, zs_ref, bt_ref, acc_ref):
    i = pl.program_id(2)
    tq = q_ref.shape[0]
    tk = up_ref.shape[0]
    assert tq == 2 * tk, "the schedule below visits the key blocks of a query tile in pairs"
    d = SB_HEAD_DIM
    x = q_ref[...]
    lane = lax.broadcasted_iota(jnp.int32, x.shape, 1)
    first = lane < d
    x2 = x * x
    ss_a = jnp.sum(jnp.where(first, x2, 0.0), axis=-1, keepdims=True)
    ss_b = jnp.sum(jnp.where(first, 0.0, x2), axis=-1, keepdims=True)
    inv = jnp.where(first, lax.rsqrt(ss_a / d + EPS), lax.rsqrt(ss_b / d + EPS))
    qn = x * inv * gq_ref[...] * (LOG2E * d ** -0.5)
    heads = (jnp.where(first, qn, 0.0).astype(BF16), jnp.where(first, 0.0, qn).astype(BF16))
    upper = up_ref[...]
    row = lax.broadcasted_iota(jnp.int32, (tq, 1), 0)
    n_blocks = 2 * i + 2

    def start(n):
        kb = jnp.maximum(n_blocks - 1 - n, 0)
        return pl.multiple_of(kb * tk, tk)

    everything = pl.ds(0, tq)
    second_half = pl.ds(tk, tk)

    def scores(n, slot, rows=everything):
        kt = kt_ref[0, 0, :, pl.ds(start(n), tk)]
        for h in range(2):
            _sb_scores(heads[h][rows.start:rows.start + rows.size], kt, z_ref.at[slot, h, rows])

    def sticks(slot, runs, first_valid_col, rows=everything):
        return tuple(_sb_sticks(z_ref.at[slot, h, rows], upper, runs[h], first_valid_col, zs_ref.at[slot, h, rows],
                                bt_ref.at[slot, h, rows]) for h in range(2))

    def values(n, slot, rows=everything):
        vb = v_ref[pl.ds(start(n), tk), :]
        for h in range(2):
            _sb_values(zs_ref.at[slot, h, rows], bt_ref.at[slot, h, rows], vb, acc_ref.at[h, rows])

    acc_ref[...] = jnp.zeros_like(acc_ref)
    half_zero = jnp.zeros((tk, 1), F32)
    scores(0, 0, second_half)
    scores(1, 1)
    runs = sticks(0, (half_zero, half_zero), row[:tk], second_half)
    runs = tuple(jnp.concatenate([half_zero, r], axis=0) for r in runs)
    scores(2, 0)
    runs = sticks(1, runs, row)
    values(0, 0, second_half)

    def body(m, runs):
        n = 2 + 2 * m
        scores(n + 1, 1)
        runs = sticks(0, runs, None)
        values(n - 1, 1)
        scores(n + 2, 0)
        runs = sticks(1, runs, None)
        values(n, 0)
        return runs

    lax.fori_loop(0, i, body, runs)
    values(n_blocks - 1, 1)
    o_ref[...] = jnp.where(first, acc_ref[0], acc_ref[1]).astype(o_ref.dtype)


def stick_breaking(qraw, g_q, kt, v, batch, seq):
    t, width = qraw.shape
    tq = SB_Q_TILE
    tk = SB_K_TILE
    nq = seq // tq
    w = V7X_LANES
    pairs = width // w
    r = lax.broadcasted_iota(jnp.int32, (tk, tk), 0)
    c = lax.broadcasted_iota(jnp.int32, (tk, tk), 1)
    upper = (r > c).astype(BF16)
    gq2 = jnp.tile(g_q.reshape(1, SB_HEAD_DIM), (1, w // SB_HEAD_DIM))
    nbytes = 2 * (tq * w * 4 + 2 * seq * w * 2 + tk * tk * 2 + tq * w * 2) + 32 * tq * tk * 4
    return pl.pallas_call(
        _sb_kernel,
        grid=(batch, pairs, nq),
        in_specs=[
            pl.BlockSpec((tq, w), lambda b, p, i: (b * nq + i, p)),
            pl.BlockSpec((1, w), lambda b, p, i: (0, 0)),
            pl.BlockSpec((1, 1, w, seq), lambda b, p, i: (b, p, 0, 0)),
            pl.BlockSpec((seq, w), lambda b, p, i: (b, p)),
            pl.BlockSpec((tk, tk), lambda b, p, i: (0, 0)),
        ],
        out_specs=pl.BlockSpec((tq, w), lambda b, p, i: (b * nq + i, p)),
        out_shape=jax.ShapeDtypeStruct((t, width), BF16),
        scratch_shapes=[
            pltpu.VMEM((2, 2, tq, tk), F32),
            pltpu.VMEM((2, 2, tq, tk), F32),
            pltpu.VMEM((2, 2, tq, tk), F32),
            pltpu.VMEM((2, tq, w), F32),
        ],
        compiler_params=_params(("parallel", "parallel", "arbitrary"), nbytes),
        name="stick_breaking",
    )(qraw, gq2, kt, v, upper)


def _ffn_kernel(x_ref, g_ref, wg_ref, wu_ref, wd_ref, o_ref, xn_ref, acc_ref):
    f = pl.program_id(1)

    @pl.when(f == 0)
    def _():
        xn_ref[...] = _rms_scale(x_ref[...], g_ref[...]).astype(BF16)
        acc_ref[...] = jnp.zeros_like(acc_ref)

    xn = xn_ref[...]
    hg = jnp.dot(xn, wg_ref[...], preferred_element_type=F32)
    hu = jnp.dot(xn, wu_ref[...], preferred_element_type=F32)
    act = (jax.nn.silu(hg) * hu).astype(BF16)
    acc_ref[...] += jnp.dot(act, wd_ref[...], preferred_element_type=F32)

    @pl.when(f == pl.num_programs(1) - 1)
    def _():
        o_ref[...] = x_ref[...] + acc_ref[...]


def ffn_dense(x, g, wg, wu, wd):
    t, d = x.shape
    f = wg.shape[1]
    tm = FFN_ROW_TILE
    tf = FFN_COL_TILE
    nbytes = 2 * (2 * tm * d * 4 + 3 * d * tf * 2) + tm * d * 6 + 4 * tm * tf * 4
    return pl.pallas_call(
        _ffn_kernel,
        grid=(t // tm, f // tf),
        in_specs=[
            pl.BlockSpec((tm, d), lambda i, j: (i, 0)),
            pl.BlockSpec((1, d), lambda i, j: (0, 0)),
            pl.BlockSpec((d, tf), lambda i, j: (0, j)),
            pl.BlockSpec((d, tf), lambda i, j: (0, j)),
            pl.BlockSpec((tf, d), lambda i, j: (j, 0)),
        ],
        out_specs=pl.BlockSpec((tm, d), lambda i, j: (i, 0)),
        out_shape=jax.ShapeDtypeStruct((t, d), F32),
        scratch_shapes=[pltpu.VMEM((tm, d), BF16), pltpu.VMEM((tm, d), F32)],
        compiler_params=_params(("parallel", "arbitrary"), nbytes),
        name="ffn_dense",
    )(x, g.reshape(1, d), wg, wu, wd)


def _router_kernel(x_ref, g_ref, wr_ref, tri_ref, xn_ref, mi_ref, mf_ref, cnt_ref, run_ref):
    @pl.when(pl.program_id(0) == 0)
    def _():
        run_ref[...] = jnp.zeros_like(run_ref)

    xn = _rms_scale(x_ref[...], g_ref[...])
    xn_ref[...] = xn
    logits = jnp.dot(xn.astype(BF16), wr_ref[...], preferred_element_type=F32)
    lane = lax.broadcasted_iota(jnp.int32, logits.shape, 1)
    neg = jnp.asarray(-jnp.inf, F32)
    logits = jnp.where(lane < N_EXPERTS, logits, neg)
    m1 = jnp.max(logits, axis=-1, keepdims=True)
    i1 = jnp.min(jnp.where(logits == m1, lane, V7X_LANES), axis=-1, keepdims=True)
    rest = jnp.where(lane == i1, neg, logits)
    m2 = jnp.max(rest, axis=-1, keepdims=True)
    i2 = jnp.min(jnp.where(rest == m2, lane, V7X_LANES), axis=-1, keepdims=True)
    e2 = jnp.exp(m2 - m1)
    denom = 1.0 + e2
    mf_ref[...] = jnp.where(lane == 0, 1.0 / denom, jnp.where(lane == 1, e2 / denom, 0.0))

    chosen = jnp.logical_or(lane == i1, lane == i2)
    onehot = jnp.where(chosen, 1.0, 0.0)
    rank = jnp.dot(tri_ref[...], onehot.astype(BF16), preferred_element_type=F32) + run_ref[...]
    r1 = jnp.sum(jnp.where(lane == i1, rank, 0.0), axis=-1, keepdims=True).astype(jnp.int32)
    r2 = jnp.sum(jnp.where(lane == i2, rank, 0.0), axis=-1, keepdims=True).astype(jnp.int32)
    mi_ref[...] = jnp.where(lane == 0, i1, jnp.where(lane == 1, i2, jnp.where(lane == 2, r1, jnp.where(lane == 3, r2, 0))))
    run_ref[...] += jnp.sum(onehot, axis=0, keepdims=True)
    cnt_ref[...] = run_ref[...]


def router(x, g, w_router_padded):
    t, d = x.shape
    tm = ROW_TILE
    w = V7X_LANES
    r = lax.broadcasted_iota(jnp.int32, (tm, tm), 0)
    c = lax.broadcasted_iota(jnp.int32, (tm, tm), 1)
    tri = (c < r).astype(BF16)
    nbytes = 2 * (2 * tm * d * 4 + d * w * 2 + tm * tm * 2 + 2 * tm * w * 4) + 2 * tm * d * 4
    row_spec = pl.BlockSpec((tm, w), lambda i: (i, 0))
    return pl.pallas_call(
        _router_kernel,
        grid=(t // tm,),
        in_specs=[
            pl.BlockSpec((tm, d), lambda i: (i, 0)),
            pl.BlockSpec((1, d), lambda i: (0, 0)),
            pl.BlockSpec((d, w), lambda i: (0, 0)),
            pl.BlockSpec((tm, tm), lambda i: (0, 0)),
        ],
        out_specs=[pl.BlockSpec((tm, d), lambda i: (i, 0)), row_spec, row_spec, pl.BlockSpec((1, w), lambda i: (0, 0))],
        out_shape=[
            jax.ShapeDtypeStruct((t, d), F32),
            jax.ShapeDtypeStruct((t, w), jnp.int32),
            jax.ShapeDtypeStruct((t, w), F32),
            jax.ShapeDtypeStruct((1, w), F32),
        ],
        scratch_shapes=[pltpu.VMEM((1, w), F32)],
        compiler_params=_params(("arbitrary",), nbytes),
        name="router",
    )(x, g.reshape(1, d), w_router_padded, tri)


def _row_copy(src, src_row, dst, dst_row, sem):
    return pltpu.make_async_copy(src.at[pl.ds(src_row, 1)], dst.at[pl.ds(dst_row, 1)], sem)


def _dispatch_kernel(slots_ref, last_ref, xn_ref, xs_hbm, zero_ref, sem, zero_sem):
    td = xn_ref.shape[0]
    tg = zero_ref.shape[0]

    @pl.when(pl.program_id(0) == 0)
    def _():
        zero_ref[...] = jnp.zeros_like(zero_ref)

        def fill(e):
            return pltpu.make_async_copy(zero_ref, xs_hbm.at[pl.ds(pl.multiple_of(last_ref[e], tg), tg)], zero_sem)

        for e in range(last_ref.shape[0]):
            @pl.when(last_ref[e] >= 0)
            def _(e=e):
                fill(e).start()

        for e in range(last_ref.shape[0]):
            @pl.when(last_ref[e] >= 0)
            def _(e=e):
                fill(e).wait()

    def body(r, carry):
        for k in range(2):
            _row_copy(xn_ref, r, xs_hbm, slots_ref[2 * r + k], sem).start(priority=k)
        return carry

    lax.fori_loop(0, td, body, 0, unroll=8)
    for k in range(2):
        pltpu.make_async_copy(xn_ref, xs_hbm.at[pl.ds(0, td)], sem).wait()


def dispatch(slots, last_tile_row, xn, n_slots):
    t, d = xn.shape
    td = MOE_ROUTE_TILE
    tg = MOE_GROUP_TILE
    return pl.pallas_call(
        _dispatch_kernel,
        grid=(t // td,),
        in_specs=[
            pl.BlockSpec((2 * td,), lambda i: (i,), memory_space=pltpu.SMEM),
            pl.BlockSpec(memory_space=pltpu.SMEM),
            pl.BlockSpec((td, d), lambda i: (i, 0)),
        ],
        out_specs=pl.BlockSpec(memory_space=pl.ANY),
        out_shape=jax.ShapeDtypeStruct((n_slots, d), xn.dtype),
        scratch_shapes=[pltpu.VMEM((tg, d), xn.dtype), pltpu.SemaphoreType.DMA, pltpu.SemaphoreType.DMA],
        compiler_params=_params(("arbitrary",), 2 * td * d * 4 + tg * d * 4),
        name="moe_dispatch",
    )(slots, last_tile_row, xn)


def _experts_kernel(te_ref, na_ref, xs_ref, wg_ref, wu_ref, wd_ref, ys_ref, xb_ref, acc_ref):
    del te_ref
    f = pl.program_id(1)
    active = pl.program_id(0) < na_ref[0]

    @pl.when(jnp.logical_and(jnp.logical_not(active), f == 0))
    def _():
        ys_ref[...] = jnp.zeros_like(ys_ref)

    @pl.when(active)
    def _():
        @pl.when(f == 0)
        def _():
            xb_ref[...] = xs_ref[...].astype(BF16)
            acc_ref[...] = jnp.zeros_like(acc_ref)

        xb = xb_ref[...]
        hg = jnp.dot(xb, wg_ref[0], preferred_element_type=F32)
        hu = jnp.dot(xb, wu_ref[0], preferred_element_type=F32)
        act = (jax.nn.silu(hg) * hu).astype(BF16)
        acc_ref[...] += jnp.dot(act, wd_ref[0], preferred_element_type=F32)

        @pl.when(f == pl.num_programs(1) - 1)
        def _():
            ys_ref[...] = acc_ref[...]


def experts(tile_expert, n_active, xs, wg, wu, wd):
    n_slots, d = xs.shape
    fe = wg.shape[2]
    tg = MOE_GROUP_TILE
    tf = MOE_COL_TILE
    nf = fe // tf
    n_tiles = n_slots // tg

    def tile(n, na):
        return jnp.minimum(n, na[0] - 1)

    def col(n, f, na):
        return jnp.where(n < na[0], f, nf - 1)

    nbytes = 2 * (2 * tg * d * 4 + 3 * d * tf * 2) + tg * d * 6 + 4 * tg * tf * 4
    return pl.pallas_call(
        _experts_kernel,
        grid_spec=pltpu.PrefetchScalarGridSpec(
            num_scalar_prefetch=2,
            grid=(n_tiles, nf),
            in_specs=[
                pl.BlockSpec((tg, d), lambda n, f, te, na: (tile(n, na), 0)),
                pl.BlockSpec((1, d, tf), lambda n, f, te, na: (te[tile(n, na)], 0, col(n, f, na))),
                pl.BlockSpec((1, d, tf), lambda n, f, te, na: (te[tile(n, na)], 0, col(n, f, na))),
                pl.BlockSpec((1, tf, d), lambda n, f, te, na: (te[tile(n, na)], col(n, f, na), 0)),
            ],
            out_specs=pl.BlockSpec((tg, d), lambda n, f, te, na: (n, 0)),
            scratch_shapes=[pltpu.VMEM((tg, d), BF16), pltpu.VMEM((tg, d), F32)],
        ),
        out_shape=jax.ShapeDtypeStruct((n_slots, d), F32),
        compiler_params=_params(("arbitrary", "arbitrary"), nbytes),
        name="moe_experts",
    )(tile_expert, n_active, xs, wg, wu, wd)


def _combine_kernel(slots_ref, next_slots_ref, x_ref, mf_ref, ys_hbm, o_ref, buf_ref, sems):
    step = pl.program_id(0)
    tc = x_ref.shape[0]

    def gather(s_ref, parity):
        def body(r, carry):
            for k in range(2):
                _row_copy(ys_hbm, s_ref[2 * r + k], buf_ref.at[parity, k], r, sems.at[parity]).start()
            return carry

        lax.fori_loop(0, tc, body, 0, unroll=8)

    @pl.when(step == 0)
    def _():
        gather(slots_ref, 0)

    lane = lax.broadcasted_iota(jnp.int32, mf_ref.shape, 1)
    mf = mf_ref[...]
    w1 = jnp.sum(jnp.where(lane == 0, mf, 0.0), axis=-1, keepdims=True)
    w2 = jnp.sum(jnp.where(lane == 1, mf, 0.0), axis=-1, keepdims=True)
    for parity in range(2):
        @pl.when(step % 2 == parity)
        def _(parity=parity):
            @pl.when(step + 1 < pl.num_programs(0))
            def _():
                gather(next_slots_ref, 1 - parity)

            for k in range(2):
                pltpu.make_async_copy(ys_hbm.at[pl.ds(0, tc)], buf_ref.at[parity, k], sems.at[parity]).wait()
            o_ref[...] = x_ref[...] + (w1 * buf_ref[parity, 0] + w2 * buf_ref[parity, 1])


def combine(slots, x, meta_f, ys):
    t, d = x.shape
    tc = MOE_ROUTE_TILE
    w = meta_f.shape[1]
    steps = t // tc
    nbytes = 2 * (2 * tc * d * 4 + tc * w * 4) + 4 * tc * d * 4
    return pl.pallas_call(
        _combine_kernel,
        grid=(steps,),
        in_specs=[
            pl.BlockSpec((2 * tc,), lambda i: (i,), memory_space=pltpu.SMEM),
            pl.BlockSpec((2 * tc,), lambda i: (jnp.minimum(i + 1, steps - 1),), memory_space=pltpu.SMEM),
            pl.BlockSpec((tc, d), lambda i: (i, 0)),
            pl.BlockSpec((tc, w), lambda i: (i, 0)),
            pl.BlockSpec(memory_space=pl.ANY),
        ],
        out_specs=pl.BlockSpec((tc, d), lambda i: (i, 0)),
        out_shape=jax.ShapeDtypeStruct((t, d), F32),
        scratch_shapes=[pltpu.VMEM((2, 2, tc, d), F32), pltpu.SemaphoreType.DMA((2,))],
        compiler_params=_params(("arbitrary",), nbytes),
        name="moe_combine",
    )(slots, slots, x, meta_f, ys)


def moe_routed(x, g, w_router, wg, wu, wd):
    t, d = x.shape
    w = V7X_LANES
    tg = MOE_GROUP_TILE
    n_tiles = 2 * t // tg + N_EXPERTS
    w_r = jnp.pad(w_router, ((0, 0), (0, w - N_EXPERTS))).astype(BF16)
    xn, meta_i, meta_f, counts = router(x, g, w_r)

    counts = counts[0, :N_EXPERTS].astype(jnp.int32)
    tiles_per = (counts + tg - 1) // tg
    tile_end = jnp.cumsum(tiles_per)
    group_start = (tile_end - tiles_per) * tg
    n_active = tile_end[-1:].astype(jnp.int32)
    tile_ids = jnp.arange(n_tiles, dtype=jnp.int32)
    tile_expert = jnp.minimum(jnp.sum(tile_ids[:, None] >= tile_end[None, :], axis=1), N_EXPERTS - 1).astype(jnp.int32)
    slots = (jnp.take(group_start, meta_i[:, 0:2]) + meta_i[:, 2:4]).reshape(-1).astype(jnp.int32)

    trailing = n_tiles - 1 - jnp.arange(N_EXPERTS, dtype=jnp.int32)
    fill_rows = jnp.concatenate([jnp.where(tiles_per > 0, (tile_end - 1) * tg, -1),
                                 jnp.where(trailing >= n_active[0], trailing * tg, -1)]).astype(jnp.int32)
    xs = dispatch(slots, fill_rows, xn, n_tiles * tg)
    ys = experts(tile_expert, n_active, xs, wg, wu, wd)
    return combine(slots, x, meta_f, ys)


def kernel(x, mix_norm, ffn_norm, mlstm_w_in, mlstm_b_igate, mlstm_b_fgate, mlstm_g_h, mlstm_w_out, kv_norm, w_kv, g_k, sb_w_q, sb_g_q, sb_w_o, ffn_w_gate, ffn_w_up, ffn_w_down, moe_w_router, moe_w_gate, moe_w_up, moe_w_down):
    batch, seq, d = x.shape
    xt = x.reshape(batch * seq, d)
    w = V7X_LANES
    n_main = 2 * MLSTM_HEADS * MLSTM_QK_DIM + 2 * MLSTM_HEADS * MLSTM_V_DIM

    w_in = mlstm_w_in[0]
    w_main = w_in[:, :n_main].astype(BF16)
    w_gate = jnp.pad(w_in[:, n_main:], ((0, 0), (0, w - 2 * MLSTM_HEADS))).astype(BF16)
    bias = jnp.pad(jnp.concatenate([mlstm_b_igate[0], mlstm_b_fgate[0]]), (0, w - 2 * MLSTM_HEADS)).reshape(1, w)
    proj = norm_matmul(xt, mix_norm[0], w_main, BF16, tn=PROJ_COL_TILE)
    igs, bcum = gate_prep(xt, mix_norm[0], w_gate, bias)
    hmix = mlstm_core(proj, igs, bcum, mlstm_g_h[0], batch, seq)
    xt = matmul_residual(hmix, mlstm_w_out[0].astype(BF16), xt)

    xt = ffn_dense(xt, ffn_norm[0], ffn_w_gate[0].astype(BF16), ffn_w_up[0].astype(BF16), ffn_w_down[0].astype(BF16))

    sb_width = SB_HEADS * SB_HEAD_DIM
    kt = key_proj(xt, kv_norm, w_kv[:, :sb_width].astype(BF16), g_k, batch, seq)
    v = norm_matmul(xt, kv_norm, w_kv[:, sb_width:].astype(BF16), BF16, tn=PROJ_COL_TILE)
    qraw = norm_matmul(xt, mix_norm[1], sb_w_q[0].astype(BF16), F32, tn=PROJ_COL_TILE)
    o = stick_breaking(qraw, sb_g_q[0], kt, v, batch, seq)
    xt = matmul_residual(o, sb_w_o[0].astype(BF16), xt)

    xt = moe_routed(xt, ffn_norm[1], moe_w_router[0], moe_w_gate[0].astype(BF16), moe_w_up[0].astype(BF16), moe_w_down[0].astype(BF16))
    return xt.reshape(batch, seq, d)
```
